```python
import math
import jax, jax.numpy as jnp
from jax import lax
import numpy as np

D_MODEL = 1024
BATCH = 8
SEQ = 4096
DEPTH = 2
DEC_BATCH = 2
DEC_SEQ = 8192
PAST_LEN = 128

HEAD_DIM = 64
NA_HEADS = D_MODEL // HEAD_DIM
DIL_HEADS = D_MODEL // HEAD_DIM
GRID_W = 64
NA_KH = 8
NA_KW = 16
NA_CB = 16
NA_BAND = NA_CB + NA_KW
DIL_CONFIGS = ((128, 1), (512, 4), (2048, 16))
DIL_BLOCK = 64
ROPE_THETA = 10000.0
PEER_HEADS = 8
PEER_QDIM = 256
N_KEYS = 128
N_EXPERTS = N_KEYS * N_KEYS
PEER_TOPK = 16
PEER_CHUNK = 128
RMS_EPS = 1e-6
NEG_INF = -1e30
N_NA_LAYERS = (DEPTH + 1) // 2
N_DIL_LAYERS = DEPTH // 2

kernel_name = "hybrid_natten_dilated_peer_encoder"


def rmsnorm(x, g):
    xf = x.astype(jnp.float32)
    y = xf * lax.rsqrt(jnp.mean(xf * xf, axis=-1, keepdims=True) + RMS_EPS)
    return (y * g.astype(jnp.float32)).astype(x.dtype)


def rope(x, pos):
    half = x.shape[-1] // 2
    freqs = ROPE_THETA ** (-jnp.arange(half, dtype=jnp.float32) / half)
    ang = pos[:, None] * freqs[None, :]
    cos = jnp.cos(ang)[None, :, None, None, :]
    sin = jnp.sin(ang)[None, :, None, None, :]
    xf = x.astype(jnp.float32)
    x1, x2 = xf[..., :half], xf[..., half:]
    return jnp.concatenate([x1 * cos - x2 * sin, x2 * cos + x1 * sin], axis=-1).astype(x.dtype)


def neighbourhood_attention(xn, w_qkv, rpb, w_o):
    B, S, _ = xn.shape
    rows = S // GRID_W
    kh = min(NA_KH, rows)
    qkv = (xn @ w_qkv).reshape(B, rows, GRID_W, 3, NA_HEADS, HEAD_DIM)
    q, k, v = qkv[:, :, :, 0], qkv[:, :, :, 1], qkv[:, :, :, 2]
    scale = HEAD_DIM ** -0.5
    n_cb = GRID_W // NA_CB
    qcol = np.arange(GRID_W).reshape(n_cb, NA_CB)
    band0 = np.clip(np.arange(n_cb) * NA_CB - NA_KW // 2, 0, GRID_W - NA_BAND)
    kcol = band0[:, None] + np.arange(NA_BAND)[None, :]
    cs = np.clip(qcol - NA_KW // 2, 0, GRID_W - NA_KW)
    col_ok = (kcol[:, None, :] >= cs[..., None]) & (kcol[:, None, :] < cs[..., None] + NA_KW)
    col_neg = jnp.asarray(np.where(col_ok, 0.0, NEG_INF), jnp.float32)
    col_off = np.clip(kcol[:, None, :] - qcol[..., None] + NA_KW - 1, 0, 2 * NA_KW - 2)
    col_bias = rpb[:, :, col_off].astype(jnp.float32)

    def row_fn(r):
        rs = jnp.clip(r - kh // 2, 0, rows - kh)
        q_r = lax.dynamic_index_in_dim(q, r, axis=1, keepdims=False)
        k_b = lax.dynamic_slice_in_dim(k, rs, kh, axis=1)
        v_b = lax.dynamic_slice_in_dim(v, rs, kh, axis=1)
        q_c = q_r.reshape(B, n_cb, NA_CB, NA_HEADS, HEAD_DIM)
        k_c = k_b[:, :, kcol]
        v_c = v_b[:, :, kcol]
        s = jnp.einsum('bnqhd,binjhd->bhnqij', q_c, k_c).astype(jnp.float32) * scale
        row_idx = rs + jnp.arange(kh) - r + NA_KH - 1
        bias = jnp.take(col_bias, row_idx, axis=1).transpose(0, 2, 3, 1, 4)
        s = s + bias[None] + col_neg[None, None, :, :, None, :]
        p = jax.nn.softmax(s.reshape(B, NA_HEADS, n_cb, NA_CB, kh * NA_BAND), axis=-1)
        p = p.reshape(s.shape).astype(v_c.dtype)
        o = jnp.einsum('bhnqij,binjhd->bnqhd', p, v_c)
        return o.reshape(B, GRID_W, NA_HEADS, HEAD_DIM)

    o = lax.map(row_fn, jnp.arange(rows))
    o = jnp.moveaxis(o, 0, 1).reshape(B, S, NA_HEADS * HEAD_DIM)
    return o @ w_o


def dilated_group(q, k, v, window, dil):
    B, S, H, hd = q.shape
    half = window // (2 * dil)
    C = DIL_BLOCK
    n = S // dil
    nb = -(-n // C)
    n_pad = nb * C
    scale = hd ** -0.5

    def to_res(t):
        return t.reshape(B, n, dil, H, hd).transpose(0, 2, 1, 3, 4)

    qr = jnp.pad(to_res(q), ((0, 0), (0, 0), (0, n_pad - n), (0, 0), (0, 0))).reshape(B, dil, nb, C, H, hd)
    pad_kv = ((0, 0), (0, 0), (C, n_pad - n + C), (0, 0), (0, 0))
    kr = jnp.pad(to_res(k), pad_kv)
    vr = jnp.pad(to_res(v), pad_kv)

    def band(t):
        return jnp.concatenate([t[:, :, s * C:s * C + n_pad].reshape(B, dil, nb, C, H, hd) for s in range(3)], axis=3)

    kb, vb = band(kr), band(vr)
    qi = np.arange(n_pad).reshape(nb, C)
    ki = (np.arange(nb)[:, None] - 1) * C + np.arange(3 * C)[None, :]
    valid = (np.abs(ki[:, None, :] - qi[:, :, None]) <= half) & (ki[:, None, :] >= 0) & (ki[:, None, :] < n)
    valid = jnp.asarray(valid)
    s = jnp.einsum('brnqhd,brnkhd->brnhqk', qr, kb).astype(jnp.float32) * scale
    s = jnp.where(valid[None, None, :, None], s, NEG_INF)
    m = jnp.max(s, axis=-1)
    e = jnp.exp(s - m[..., None])
    l = jnp.sum(e, axis=-1)
    o = jnp.einsum('brnhqk,brnkhd->brnqhd', e.astype(vb.dtype), vb).astype(jnp.float32)
    o = o / jnp.swapaxes(l, 3, 4)[..., None]
    lse = jnp.swapaxes(m + jnp.log(l), 3, 4)
    o = o.reshape(B, dil, n_pad, H, hd)[:, :, :n].transpose(0, 2, 1, 3, 4).reshape(B, S, H, hd)
    lse = lse.reshape(B, dil, n_pad, H)[:, :, :n].transpose(0, 2, 1, 3).reshape(B, S, H)
    return o, lse


def dilated_attention(xn, w_qkv, w_o):
    B, S, _ = xn.shape
    G = len(DIL_CONFIGS)
    qkv = (xn @ w_qkv).reshape(B, S, G, 3, DIL_HEADS, HEAD_DIM)
    pos = jnp.arange(S, dtype=jnp.float32)
    q = rope(qkv[:, :, :, 0], pos)
    k = rope(qkv[:, :, :, 1], pos)
    v = qkv[:, :, :, 2]
    outs, lses = [], []
    for g, (window, dil) in enumerate(DIL_CONFIGS):
        o_g, lse_g = dilated_group(q[:, :, g], k[:, :, g], v[:, :, g], window, dil)
        outs.append(o_g)
        lses.append(lse_g)
    wgt = jax.nn.softmax(jnp.stack(lses, axis=0), axis=0)
    o = jnp.sum(wgt[..., None] * jnp.stack(outs, axis=0), axis=0).astype(xn.dtype)
    return o.reshape(B, S, DIL_HEADS * HEAD_DIM) @ w_o


def peer(xn, w_query, sk1, sk2, u, v):
    B, S, D = xn.shape
    T = B * S
    xc = xn.reshape(T // PEER_CHUNK, PEER_CHUNK, D)
    hq = PEER_QDIM // 2

    def chunk_fn(xt):
        q = (xt @ w_query).reshape(PEER_CHUNK, PEER_HEADS, PEER_QDIM).astype(jnp.float32)
        s1 = jnp.einsum('thc,hkc->thk', q[..., :hq], sk1.astype(jnp.float32))
        s2 = jnp.einsum('thc,hkc->thk', q[..., hq:], sk2.astype(jnp.float32))
        v1, i1 = lax.top_k(s1, PEER_TOPK)
        v2, i2 = lax.top_k(s2, PEER_TOPK)
        cand = (v1[..., :, None] + v2[..., None, :]).reshape(PEER_CHUNK, PEER_HEADS, PEER_TOPK * PEER_TOPK)
        sc, ci = lax.top_k(cand, PEER_TOPK)
        e_idx = (jnp.take_along_axis(i1, ci // PEER_TOPK, axis=-1) * N_KEYS
                 + jnp.take_along_axis(i2, ci % PEER_TOPK, axis=-1))
        g = jax.nn.softmax(sc, axis=-1)
        a = jax.nn.gelu(jnp.einsum('td,thkd->thk', xt, u[e_idx]).astype(jnp.float32), approximate=False)
        return jnp.einsum('thk,thkd->td', (g * a).astype(xt.dtype), v[e_idx])

    return lax.map(chunk_fn, xc).reshape(B, S, D)


def setup_inputs(seed: int = 0) -> dict:
    key = jax.random.key(seed)
    ks = jax.random.split(key, 16)

    def nrm(k, shape, scale):
        return jax.random.normal(k, shape, jnp.float32) * scale

    D = D_MODEL
    return {
        "x_prompt": nrm(ks[0], (BATCH, SEQ, D), 1.0),
        "x_sample": nrm(ks[1], (DEC_BATCH, DEC_SEQ, D), 1.0),
        "ln_mix": 1.0 + nrm(ks[2], (DEPTH, D), 0.05),
        "ln_ffn": 1.0 + nrm(ks[3], (DEPTH, D), 0.05),
        "ln_final": 1.0 + nrm(ks[4], (D,), 0.05),
        "w_qkv_na": nrm(ks[5], (N_NA_LAYERS, D, 3 * NA_HEADS * HEAD_DIM), D ** -0.5),
        "rpb_na": nrm(ks[6], (N_NA_LAYERS, NA_HEADS, 2 * NA_KH - 1, 2 * NA_KW - 1), 0.5),
        "w_o_na": nrm(ks[7], (N_NA_LAYERS, NA_HEADS * HEAD_DIM, D), (NA_HEADS * HEAD_DIM) ** -0.5),
        "w_qkv_dil": nrm(ks[8], (N_DIL_LAYERS, D, len(DIL_CONFIGS) * 3 * DIL_HEADS * HEAD_DIM), D ** -0.5),
        "w_o_dil": nrm(ks[9], (N_DIL_LAYERS, DIL_HEADS * HEAD_DIM, D), (DIL_HEADS * HEAD_DIM) ** -0.5),
        "w_query_peer": nrm(ks[10], (DEPTH, D, PEER_HEADS * PEER_QDIM), D ** -0.5),
        "subkeys1_peer": nrm(ks[11], (DEPTH, PEER_HEADS, N_KEYS, PEER_QDIM // 2), (PEER_QDIM // 2) ** -0.5),
        "subkeys2_peer": nrm(ks[12], (DEPTH, PEER_HEADS, N_KEYS, PEER_QDIM // 2), (PEER_QDIM // 2) ** -0.5),
        "u_peer": nrm(ks[13], (DEPTH, N_EXPERTS, D), D ** -0.5),
        "v_peer": nrm(ks[14], (DEPTH, N_EXPERTS, D), PEER_HEADS ** -0.5),
    }


def reference(x_prompt, x_sample, ln_mix, ln_ffn, ln_final, w_qkv_na, rpb_na, w_o_na,
              w_qkv_dil, w_o_dil, w_query_peer, subkeys1_peer, subkeys2_peer, u_peer, v_peer):
    def trunk(x):
        for layer in range(DEPTH):
            h = rmsnorm(x, ln_mix[layer])
            if layer % 2 == 0:
                j = layer // 2
                x = x + neighbourhood_attention(h, w_qkv_na[j], rpb_na[j], w_o_na[j])
            else:
                j = layer // 2
                x = x + dilated_attention(h, w_qkv_dil[j], w_o_dil[j])
            h = rmsnorm(x, ln_ffn[layer])
            x = x + peer(h, w_query_peer[layer], subkeys1_peer[layer], subkeys2_peer[layer],
                         u_peer[layer], v_peer[layer])
        return rmsnorm(x, ln_final)

    y_prompt = trunk(x_prompt)
    y_sample = trunk(x_sample)
    return (y_prompt, y_sample)
```

```python
import functools

import numpy as np
import jax
import jax.numpy as jnp
from jax import lax
from jax.experimental import pallas as pl
from jax.experimental.pallas import tpu as pltpu

D_MODEL = 1024
HEAD_DIM = 64
N_HEADS = 16
GRID_W = 64
NA_KH = 8
NA_KW = 16
NA_ROWS_PER_STEP = 4
DIL_CONFIGS = ((128, 1), (512, 4), (2048, 16))
DIL_BLOCK = 128
ROPE_THETA = 10000.0
PEER_HEADS = 8
N_KEYS = 128
N_EXPERTS = N_KEYS * N_KEYS
PEER_TOPK = 16
RMS_EPS = 1e-6
NEG_INF = -1e30
LANES = 128

_MXU = jnp.bfloat16
_GATE = jnp.bfloat16
_VMEM_LIMIT = 48 * 1024 * 1024

_NT = (((1,), (1,)), ((), ()))


def _params(sem):
    return pltpu.CompilerParams(dimension_semantics=sem, vmem_limit_bytes=_VMEM_LIMIT)


def _rmsnorm(x, g):
    ms = jnp.mean(x * x, axis=-1, keepdims=True)
    return x * lax.rsqrt(ms + RMS_EPS) * g


def _norm_proj_kernel(*refs, rope):
    if rope:
        x_ref, g_ref, w_ref, cos_ref, sin_ref, o_ref, xn_ref = refs
    else:
        x_ref, g_ref, w_ref, o_ref, xn_ref = refs
    j = pl.program_id(1)

    @pl.when(j == 0)
    def _():
        xn_ref[...] = _rmsnorm(x_ref[...], g_ref[...]).astype(xn_ref.dtype)

    acc = jnp.dot(xn_ref[...], w_ref[...], preferred_element_type=jnp.float32)
    kind = j % 3

    @pl.when(kind == 2)
    def _():
        o_ref[...] = acc.astype(o_ref.dtype)

    @pl.when(kind != 2)
    def _():
        scale = jnp.where(kind == 0, HEAD_DIM ** -0.5, 1.0).astype(jnp.float32)
        if rope:
            cos = cos_ref[...]
            sin = sin_ref[...]
            lane = lax.broadcasted_iota(jnp.int32, cos.shape, 1)
            first_half = (lane % HEAD_DIM) < HEAD_DIM // 2
            for t in range(acc.shape[1] // LANES):
                xt = acc[:, t * LANES:(t + 1) * LANES]
                partner = jnp.where(first_half,
                                    pltpu.roll(xt, LANES - HEAD_DIM // 2, 1),
                                    pltpu.roll(xt, HEAD_DIM // 2, 1))
                y = xt * cos + partner * sin
                o_ref[:, t * LANES:(t + 1) * LANES] = (y * scale).astype(o_ref.dtype)
        else:
            o_ref[...] = (acc * scale).astype(o_ref.dtype)


def _norm_proj(x2, g, w, seq_len, rope_tables=None, tm=512, tn=1024):
    T, D = x2.shape
    N = w.shape[1]
    rope = rope_tables is not None
    in_specs = [
        pl.BlockSpec((tm, D), lambda i, j: (i, 0)),
        pl.BlockSpec((1, D), lambda i, j: (0, 0)),
        pl.BlockSpec((D, tn), lambda i, j: (0, j)),
    ]
    args = [x2, g, w]
    if rope:
        nper = seq_len // tm
        tab = pl.BlockSpec((tm, LANES), lambda i, j: (i % nper, 0))
        in_specs += [tab, tab]
        args += list(rope_tables)
    return pl.pallas_call(
        functools.partial(_norm_proj_kernel, rope=rope),
        grid=(T // tm, N // tn),
        in_specs=in_specs,
        out_specs=pl.BlockSpec((tm, tn), lambda i, j: (i, j)),
        out_shape=jax.ShapeDtypeStruct((T, N), _MXU),
        scratch_shapes=[pltpu.VMEM((tm, D), _MXU)],
        compiler_params=_params(("parallel", "arbitrary")),
        name="norm_proj_rope" if rope else "norm_proj",
    )(*args)


def _rope_tables(seq_len):
    half = HEAD_DIM // 2
    freqs = ROPE_THETA ** (-jnp.arange(half, dtype=jnp.float32) / half)
    ang = jnp.arange(seq_len, dtype=jnp.float32)[:, None] * freqs[None, :]
    cos = jnp.cos(ang)
    sin = jnp.sin(ang)
    cos_t = jnp.tile(cos, (1, LANES // half))
    sin_t = jnp.tile(jnp.concatenate([-sin, sin], axis=1), (1, LANES // HEAD_DIM))
    return cos_t, sin_t


def _na_kernel(q_ref, k0_ref, k1_ref, k2_ref, v0_ref, v1_ref, v2_ref, bias_ref, o_ref,
               kbuf, vbuf, *, rows):
    g = pl.program_id(2)
    rp = NA_ROWS_PER_STEP
    blk = rp * GRID_W
    nblk = rows // rp
    b0 = jnp.clip(g - 1, 0, nblk - 3)
    for i, (kr, vr) in enumerate(((k0_ref, v0_ref), (k1_ref, v1_ref), (k2_ref, v2_ref))):
        kbuf[i * blk:(i + 1) * blk, :] = kr[0]
        vbuf[i * blk:(i + 1) * blk, :] = vr[0]
    nkeys = NA_KH * GRID_W
    for r in range(rp):
        r_abs = g * rp + r
        rs = jnp.clip(r_abs - NA_KH // 2, 0, rows - NA_KH)
        off = pl.multiple_of((rs - b0 * rp) * GRID_W, GRID_W)
        var = rs - r_abs + NA_KH - 1
        for h in range(LANES // HEAD_DIM):
            hs = slice(h * HEAD_DIM, (h + 1) * HEAD_DIM)
            qh = q_ref[0, r * GRID_W:(r + 1) * GRID_W, hs]
            kh = kbuf[pl.ds(off, nkeys), hs]
            vh = vbuf[pl.ds(off, nkeys), hs]
            s = lax.dot_general(qh, kh, _NT, preferred_element_type=jnp.float32)
            s = s + bias_ref[var, h]
            m = jnp.max(s, axis=-1, keepdims=True)
            e = jnp.exp(s - m)
            l = jnp.sum(e, axis=-1, keepdims=True)
            o = jnp.dot(e.astype(vh.dtype), vh, preferred_element_type=jnp.float32) / l
            o_ref[0, r * GRID_W:(r + 1) * GRID_W, hs] = o.astype(o_ref.dtype)


def _na_bias_table(rpb):
    qc = np.arange(GRID_W)[:, None]
    kc = np.arange(GRID_W)[None, :]
    cs = np.clip(qc - NA_KW // 2, 0, GRID_W - NA_KW)
    ok = (kc >= cs) & (kc < cs + NA_KW)
    col_neg = jnp.asarray(np.where(ok, 0.0, NEG_INF), jnp.float32)
    col_off = np.clip(kc - qc + NA_KW - 1, 0, 2 * NA_KW - 2)
    tab = rpb.astype(jnp.float32)[:, :, col_off] + col_neg[None, None]
    variants = []
    for var in range(NA_KH):
        t = tab[:, var:var + NA_KH]
        variants.append(jnp.transpose(t, (0, 2, 1, 3)).reshape(N_HEADS, GRID_W, NA_KH * GRID_W))
    return jnp.stack(variants, axis=0)


def _na_attention(qkv, bias_tab, B, S):
    rows = S // GRID_W
    rp = NA_ROWS_PER_STEP
    blk = rp * GRID_W
    nblk = rows // rp
    nhp = D_MODEL // LANES
    hp_heads = LANES // HEAD_DIM

    def kv_spec(which, i):
        return pl.BlockSpec(
            (1, blk, LANES),
            lambda hp, b, g: (b, jnp.clip(g - 1, 0, nblk - 3) + i, which * nhp + hp))

    return pl.pallas_call(
        functools.partial(_na_kernel, rows=rows),
        grid=(nhp, B, nblk),
        in_specs=[pl.BlockSpec((1, blk, LANES), lambda hp, b, g: (b, g, hp))]
        + [kv_spec(1, i) for i in range(3)] + [kv_spec(2, i) for i in range(3)]
        + [pl.BlockSpec((NA_KH, hp_heads, GRID_W, NA_KH * GRID_W), lambda hp, b, g: (0, hp, 0, 0))],
        out_specs=pl.BlockSpec((1, blk, LANES), lambda hp, b, g: (b, g, hp)),
        out_shape=jax.ShapeDtypeStruct((B, S, D_MODEL), qkv.dtype),
        scratch_shapes=[pltpu.VMEM((3 * blk, LANES), qkv.dtype),
                        pltpu.VMEM((3 * blk, LANES), qkv.dtype)],
        compiler_params=_params(("arbitrary", "arbitrary", "arbitrary")),
        name="na_attention",
    )(qkv, qkv, qkv, qkv, qkv, qkv, qkv, bias_tab)


def _proj_res_kernel(o_ref, w_ref, x_ref, out_ref):
    out_ref[...] = x_ref[...] + jnp.dot(o_ref[...], w_ref[...], preferred_element_type=jnp.float32)


def _proj_residual(o2, w, x2, tm=512):
    T, D = x2.shape
    return pl.pallas_call(
        _proj_res_kernel,
        grid=(T // tm,),
        in_specs=[pl.BlockSpec((tm, D), lambda i: (i, 0)),
                  pl.BlockSpec((D, D), lambda i: (0, 0)),
                  pl.BlockSpec((tm, D), lambda i: (i, 0))],
        out_specs=pl.BlockSpec((tm, D), lambda i: (i, 0)),
        out_shape=jax.ShapeDtypeStruct((T, D), jnp.float32),
        compiler_params=_params(("parallel",)),
        name="proj_residual",
    )(o2, w, x2)


def _dil_kernel(q_ref, kp_ref, kc_ref, kn_ref, vp_ref, vc_ref, vn_ref, o_ref, lse_ref, *,
                n_blocks, half):
    jb = pl.program_id(2)
    C = q_ref.shape[1]
    qi = lax.broadcasted_iota(jnp.int32, (C, C), 0)
    ci = lax.broadcasted_iota(jnp.int32, (C, C), 1)
    diff = ci - qi
    mask_c = jnp.abs(diff) <= half
    mask_p = (diff - C >= -half) & (jb > 0)
    mask_n = (diff + C <= half) & (jb < n_blocks - 1)
    lane = lax.broadcasted_iota(jnp.int32, (C, LANES), 1)
    lse_all = jnp.zeros((C, LANES), jnp.float32)
    for h in range(N_HEADS):
        hs = slice(h * HEAD_DIM, (h + 1) * HEAD_DIM)
        qh = q_ref[0, :, hs]
        sp = lax.dot_general(qh, kp_ref[0, :, hs], _NT, preferred_element_type=jnp.float32)
        sc = lax.dot_general(qh, kc_ref[0, :, hs], _NT, preferred_element_type=jnp.float32)
        sn = lax.dot_general(qh, kn_ref[0, :, hs], _NT, preferred_element_type=jnp.float32)
        sp = jnp.where(mask_p, sp, NEG_INF)
        sc = jnp.where(mask_c, sc, NEG_INF)
        sn = jnp.where(mask_n, sn, NEG_INF)
        m = jnp.maximum(jnp.maximum(jnp.max(sp, axis=-1, keepdims=True),
                                    jnp.max(sc, axis=-1, keepdims=True)),
                        jnp.max(sn, axis=-1, keepdims=True))
        ep = jnp.exp(sp - m)
        ec = jnp.exp(sc - m)
        en = jnp.exp(sn - m)
        l = (jnp.sum(ep, axis=-1, keepdims=True) + jnp.sum(ec, axis=-1, keepdims=True)
             + jnp.sum(en, axis=-1, keepdims=True))
        dt = vp_ref.dtype
        o = (jnp.dot(ep.astype(dt), vp_ref[0, :, hs], preferred_element_type=jnp.float32)
             + jnp.dot(ec.astype(dt), vc_ref[0, :, hs], preferred_element_type=jnp.float32)
             + jnp.dot(en.astype(dt), vn_ref[0, :, hs], preferred_element_type=jnp.float32))
        o_ref[0, :, hs] = (o / l).astype(o_ref.dtype)
        lse_all = jnp.where(lane == h, m + jnp.log(l), lse_all)
    lse_ref[0] = lse_all


def _dil_group(qkv, B, S, group, window, dil):
    n = S // dil
    C = DIL_BLOCK
    nb = n // C
    ncol = qkv.shape[-1] // D_MODEL
    qkv_v = qkv.reshape(B, n, dil * qkv.shape[-1])
    base = group * 3

    def spec(which, shift):
        def idx(b, r, jb):
            return (b, jnp.clip(jb + shift, 0, nb - 1), r * ncol + base + which)
        return pl.BlockSpec((1, C, D_MODEL), idx)

    o, lse = pl.pallas_call(
        functools.partial(_dil_kernel, n_blocks=nb, half=window // (2 * dil)),
        grid=(B, dil, nb),
        in_specs=[spec(0, 0), spec(1, -1), spec(1, 0), spec(1, 1), spec(2, -1), spec(2, 0), spec(2, 1)],
        out_specs=[pl.BlockSpec((1, C, D_MODEL), lambda b, r, jb: (b, jb, r)),
                   pl.BlockSpec((1, C, LANES), lambda b, r, jb: (b, jb, r))],
        out_shape=[jax.ShapeDtypeStruct((B, n, dil * D_MODEL), jnp.float32),
                   jax.ShapeDtypeStruct((B, n, dil * LANES), jnp.float32)],
        compiler_params=_params(("parallel", "parallel", "arbitrary")),
        name=f"dil_group{group}",
    )(qkv_v, qkv_v, qkv_v, qkv_v, qkv_v, qkv_v, qkv_v)
    return o.reshape(B * S, D_MODEL), lse.reshape(B * S, LANES)


def _dil_merge_kernel(o1_ref, o2_ref, o3_ref, l1_ref, l2_ref, l3_ref, x_ref, w_ref, exp_ref, out_ref):
    lses = [l1_ref[...], l2_ref[...], l3_ref[...]]
    m = jnp.maximum(jnp.maximum(lses[0], lses[1]), lses[2])
    es = [jnp.exp(l - m) for l in lses]
    z = es[0] + es[1] + es[2]
    merged = None
    for e, o_ref in zip(es, (o1_ref, o2_ref, o3_ref)):
        w = e / z
        hi = w.astype(_MXU)
        lo = (w - hi.astype(jnp.float32)).astype(_MXU)
        wide = (jnp.dot(hi, exp_ref[...], preferred_element_type=jnp.float32)
                + jnp.dot(lo, exp_ref[...], preferred_element_type=jnp.float32))
        term = wide * o_ref[...]
        merged = term if merged is None else merged + term
    out_ref[...] = x_ref[...] + jnp.dot(merged.astype(_MXU), w_ref[...],
                                        preferred_element_type=jnp.float32)


def _dil_merge_proj(outs, lses, x2, w, tm=256):
    T, D = x2.shape
    expand = np.zeros((LANES, D), np.float32)
    for h in range(N_HEADS):
        expand[h, h * HEAD_DIM:(h + 1) * HEAD_DIM] = 1.0
    expand = jnp.asarray(expand, _MXU)
    row = pl.BlockSpec((tm, D), lambda i: (i, 0))
    lrow = pl.BlockSpec((tm, LANES), lambda i: (i, 0))
    return pl.pallas_call(
        _dil_merge_kernel,
        grid=(T // tm,),
        in_specs=[row, row, row, lrow, lrow, lrow, row,
                  pl.BlockSpec((D, D), lambda i: (0, 0)),
                  pl.BlockSpec((LANES, D), lambda i: (0, 0))],
        out_specs=row,
        out_shape=jax.ShapeDtypeStruct((T, D), jnp.float32),
        compiler_params=_params(("parallel",)),
        name="dil_merge_proj",
    )(*outs, *lses, x2, w, expand)


def _top16_rows(s):
    K, L = s.shape
    kio = lax.broadcasted_iota(jnp.int32, (K, L), 0)
    aio = lax.broadcasted_iota(jnp.int32, (PEER_TOPK, L), 0)
    vals = jnp.zeros((PEER_TOPK, L), jnp.float32)
    rank = jnp.full((K, L), PEER_TOPK, jnp.int32)
    for a in range(PEER_TOPK):
        m = jnp.max(s, axis=0, keepdims=True)
        idx = jnp.min(jnp.where(s == m, kio, K), axis=0, keepdims=True)
        sel = kio == idx
        rank = jnp.where(sel, a, rank)
        s = jnp.where(sel, -jnp.inf, s)
        vals = jnp.where(aio == a, m, vals)
    return vals, rank


def _pair_counts(v1, v2):
    L = v1.shape[1]
    K = PEER_TOPK
    blocks = [v1 + v2[0:1]]
    for b in range(1, 8):
        blocks.append(v1[0:8] + v2[b:b + 1])
    blocks.append(v1[0:1] + v2[8:16])
    cand = jnp.concatenate(blocks, axis=0)
    R = cand.shape[0]
    r = lax.broadcasted_iota(jnp.int32, (R, L), 0)
    a_mid = (r - 16) & 7
    b_mid = ((r - 16) >> 3) + 1
    a_of = jnp.where(r < 16, r, jnp.where(r < 72, a_mid, 0))
    b_of = jnp.where(r < 16, 0, jnp.where(r < 72, b_mid, r - 64))
    flat = a_of * K + b_of
    cand = jnp.where((a_of + 1) * (b_of + 1) <= K, cand, -jnp.inf)
    aio = lax.broadcasted_iota(jnp.int32, (K, L), 0)
    cnt = jnp.zeros((K, L), jnp.float32)
    top = v1[0:1] + v2[0:1]
    z = jnp.zeros((1, L), jnp.float32)
    for _ in range(K):
        m = jnp.max(cand, axis=0, keepdims=True)
        f = jnp.min(jnp.where(cand == m, flat, K * K), axis=0, keepdims=True)
        cand = jnp.where(flat == f, -jnp.inf, cand)
        cnt = cnt + jnp.where(aio == (f >> 4), 1.0, 0.0)
        z = z + jnp.exp(m - top)
    return cnt, z


def _router_kernel(x_ref, g_ref, wq_ref, k1h_ref, k1l_ref, k2h_ref, k2l_ref,
                   hn_ref, row_ref, rank2_ref, e2_ref):
    hn = _rmsnorm(x_ref[...], g_ref[...]).astype(hn_ref.dtype)
    hn_ref[...] = hn
    q = jnp.dot(hn, wq_ref[...], preferred_element_type=jnp.float32)
    q_hi = q.astype(_MXU)
    q_lo = (q - q_hi.astype(jnp.float32)).astype(_MXU)
    half = N_KEYS

    def scores(kh_ref, kl_ref, h, c0):
        qh = q_hi[:, c0:c0 + half]
        ql = q_lo[:, c0:c0 + half]
        return (lax.dot_general(kh_ref[h], qh, _NT, preferred_element_type=jnp.float32)
                + lax.dot_general(kh_ref[h], ql, _NT, preferred_element_type=jnp.float32)
                + lax.dot_general(kl_ref[h], qh, _NT, preferred_element_type=jnp.float32))

    for h in range(PEER_HEADS):
        s1 = scores(k1h_ref, k1l_ref, h, h * 2 * half)
        s2 = scores(k2h_ref, k2l_ref, h, h * 2 * half + half)
        v1, rank1 = _top16_rows(s1)
        v2, rank2 = _top16_rows(s2)
        cnt, z = _pair_counts(v1, v2)
        cnt_i = jnp.zeros_like(s1)
        for a in range(PEER_TOPK):
            cnt_i = jnp.where(rank1 == a, cnt[a:a + 1], cnt_i)
        e1 = jnp.where(rank1 < PEER_TOPK, jnp.exp(s1 - v1[0:1]), 0.0)
        e2 = jnp.where(rank2 < PEER_TOPK, jnp.exp(s2 - v2[0:1]) / z, 0.0)
        row_ref[2 * h] = cnt_i
        row_ref[2 * h + 1] = e1
        rank2_ref[h] = rank2.astype(jnp.float32).astype(rank2_ref.dtype)
        e2_ref[h] = e2.astype(e2_ref.dtype)


def _peer_router(x2, g, wq, k1h, k1l, k2h, k2l, tm=256):
    T, D = x2.shape
    NQ = wq.shape[1]
    H = PEER_HEADS
    kspec = pl.BlockSpec((H, N_KEYS, N_KEYS), lambda i: (0, 0, 0))
    return pl.pallas_call(
        _router_kernel,
        grid=(T // tm,),
        in_specs=[pl.BlockSpec((tm, D), lambda i: (i, 0)),
                  pl.BlockSpec((1, D), lambda i: (0, 0)),
                  pl.BlockSpec((D, NQ), lambda i: (0, 0)),
                  kspec, kspec, kspec, kspec],
        out_specs=[pl.BlockSpec((tm, D), lambda i: (i, 0)),
                   pl.BlockSpec((2 * H, N_KEYS, tm), lambda i: (0, 0, i)),
                   pl.BlockSpec((H, N_KEYS, tm), lambda i: (0, 0, i)),
                   pl.BlockSpec((H, N_KEYS, tm), lambda i: (0, 0, i))],
        out_shape=[jax.ShapeDtypeStruct((T, D), _MXU),
                   jax.ShapeDtypeStruct((2 * H, N_KEYS, T), jnp.float32),
                   jax.ShapeDtypeStruct((H, N_KEYS, T), _GATE),
                   jax.ShapeDtypeStruct((H, N_KEYS, T), _GATE)],
        compiler_params=_params(("parallel",)),
        name="peer_router",
    )(x2, g, wq, k1h, k1l, k2h, k2l)


def _gelu(a):
    return 0.5 * a * (1.0 + lax.erf(a * (2.0 ** -0.5)))


def _peer_dense_kernel(*refs, final_norm):
    if final_norm:
        hn_ref, u_ref, vt_ref, row_ref, rank2_ref, e2_ref, x_ref, gf_ref, out_ref, acc_ref, at_ref, gt_ref = refs
    else:
        hn_ref, u_ref, vt_ref, row_ref, rank2_ref, e2_ref, x_ref, out_ref, acc_ref, at_ref, gt_ref = refs
    e = pl.program_id(1)
    eb = u_ref.shape[0]
    sub = eb // N_KEYS

    @pl.when(e == 0)
    def _():
        acc_ref[...] = jnp.zeros_like(acc_ref)

    at_ref[...] = lax.dot_general(u_ref[...], hn_ref[...], _NT, preferred_element_type=jnp.float32)

    def body(ii, carry):
        i = e * sub + ii
        r0 = pl.multiple_of(ii * N_KEYS, N_KEYS)
        act = _gelu(at_ref[pl.ds(r0, N_KEYS), :])
        gate = None
        for h in range(PEER_HEADS):
            cnt = row_ref[2 * h, pl.ds(i, 1), :].astype(_GATE)
            e1 = row_ref[2 * h + 1, pl.ds(i, 1), :].astype(_GATE)
            term = jnp.where(rank2_ref[h] < cnt, e2_ref[h], jnp.zeros((), _GATE)) * e1
            gate = term if gate is None else gate + term
        gt_ref[pl.ds(r0, N_KEYS), :] = (gate * act.astype(_GATE)).astype(gt_ref.dtype)
        return carry

    lax.fori_loop(0, sub, body, 0)
    acc_ref[...] += jnp.dot(vt_ref[...], gt_ref[...], preferred_element_type=jnp.float32)

    @pl.when(e == pl.num_programs(1) - 1)
    def _():
        y = x_ref[...] + acc_ref[...].T
        if final_norm:
            y = _rmsnorm(y, gf_ref[...])
        out_ref[...] = y


def _peer_dense(hn, u, vt, rowdat, rank2, e2, x2, g_final=None, tm=512, eb=1024):
    T, D = x2.shape
    H = PEER_HEADS
    final_norm = g_final is not None
    in_specs = [pl.BlockSpec((tm, D), lambda t, e: (t, 0)),
                pl.BlockSpec((eb, D), lambda t, e: (e, 0)),
                pl.BlockSpec((D, eb), lambda t, e: (0, e)),
                pl.BlockSpec((2 * H, N_KEYS, tm), lambda t, e: (0, 0, t)),
                pl.BlockSpec((H, N_KEYS, tm), lambda t, e: (0, 0, t)),
                pl.BlockSpec((H, N_KEYS, tm), lambda t, e: (0, 0, t)),
                pl.BlockSpec((tm, D), lambda t, e: (t, 0))]
    args = [hn, u, vt, rowdat, rank2, e2, x2]
    if final_norm:
        in_specs.append(pl.BlockSpec((1, D), lambda t, e: (0, 0)))
        args.append(g_final)
    return pl.pallas_call(
        functools.partial(_peer_dense_kernel, final_norm=final_norm),
        grid=(T // tm, N_EXPERTS // eb),
        in_specs=in_specs,
        out_specs=pl.BlockSpec((tm, D), lambda t, e: (t, 0)),
        out_shape=jax.ShapeDtypeStruct((T, D), jnp.float32),
        scratch_shapes=[pltpu.VMEM((D, tm), jnp.float32),
                        pltpu.VMEM((eb, tm), jnp.float32),
                        pltpu.VMEM((eb, tm), _MXU)],
        compiler_params=_params(("parallel", "arbitrary")),
        name="peer_dense_final" if final_norm else "peer_dense",
    )(*args)


def _split_hi_lo(w):
    hi = w.astype(_MXU)
    lo = (w - hi.astype(jnp.float32)).astype(_MXU)
    return hi, lo


def _peer(x2, g, wq, sk1, sk2, u, vt, g_final=None):
    k1h, k1l = _split_hi_lo(sk1)
    k2h, k2l = _split_hi_lo(sk2)
    hn, rowdat, rank2, e2 = _peer_router(x2, g, wq, k1h, k1l, k2h, k2l)
    return _peer_dense(hn, u, vt, rowdat, rank2, e2, x2, g_final)


def _trunk(x, p):
    B, S, D = x.shape
    x2 = x.reshape(B * S, D)
    qkv = _norm_proj(x2, p["ln_mix"][0:1], p["w_qkv_na"], S)
    o = _na_attention(qkv.reshape(B, S, 3 * D), p["na_bias"], B, S)
    x2 = _proj_residual(o.reshape(B * S, D), p["w_o_na"], x2)
    x2 = _peer(x2, p["ln_ffn"][0:1], p["wq"][0], p["sk1"][0], p["sk2"][0], p["u"][0], p["vt"][0])
    qkv = _norm_proj(x2, p["ln_mix"][1:2], p["w_qkv_dil"], S, rope_tables=_rope_tables(S))
    qkv = qkv.reshape(B, S, qkv.shape[-1])
    outs, lses = [], []
    for gi, (window, dil) in enumerate(DIL_CONFIGS):
        o_g, lse_g = _dil_group(qkv, B, S, gi, window, dil)
        outs.append(o_g)
        lses.append(lse_g)
    x2 = _dil_merge_proj(outs, lses, x2, p["w_o_dil"])
    x2 = _peer(x2, p["ln_ffn"][1:2], p["wq"][1], p["sk1"][1], p["sk2"][1], p["u"][1], p["vt"][1],
               g_final=p["ln_final"])
    return x2.reshape(B, S, D)


def kernel(x_prompt, x_sample, ln_mix, ln_ffn, ln_final, w_qkv_na, rpb_na, w_o_na, w_qkv_dil, w_o_dil,
           w_query_peer, subkeys1_peer, subkeys2_peer, u_peer, v_peer):
    p = {
        "ln_mix": ln_mix, "ln_ffn": ln_ffn, "ln_final": ln_final.reshape(1, -1),
        "w_qkv_na": w_qkv_na[0].astype(_MXU), "w_o_na": w_o_na[0].astype(_MXU),
        "na_bias": _na_bias_table(rpb_na[0]),
        "w_qkv_dil": w_qkv_dil[0].astype(_MXU), "w_o_dil": w_o_dil[0].astype(_MXU),
        "wq": w_query_peer.astype(_MXU), "sk1": subkeys1_peer, "sk2": subkeys2_peer,
        "u": u_peer.astype(_MXU), "vt": jnp.swapaxes(v_peer, 1, 2).astype(_MXU),
    }
    return _trunk(x_prompt, p), _trunk(x_sample, p)
```

```python
import functools

import numpy as np
import jax
import jax.numpy as jnp
from jax import lax
from jax.experimental import pallas as pl
from jax.experimental.pallas import tpu as pltpu

D_MODEL = 1024
HEAD_DIM = 64
N_HEADS = 16
GRID_W = 64
NA_KH = 8
NA_KW = 16
NA_ROWS_PER_STEP = 4
DIL_CONFIGS = ((128, 1), (512, 4), (2048, 16))
DIL_BLOCK = 128
ROPE_THETA = 10000.0
PEER_HEADS = 8
N_KEYS = 128
N_EXPERTS = N_KEYS * N_KEYS
PEER_TOPK = 16
RMS_EPS = 1e-6
NEG_INF = -1e30
LANES = 128

_MXU = jnp.bfloat16
_GATE = jnp.bfloat16
_VMEM_LIMIT = 48 * 1024 * 1024

_NT = (((1,), (1,)), ((), ()))


def _params(sem):
    return pltpu.CompilerParams(dimension_semantics=sem, vmem_limit_bytes=_VMEM_LIMIT)


def _rmsnorm(x, g):
    ms = jnp.mean(x * x, axis=-1, keepdims=True)
    return x * lax.rsqrt(ms + RMS_EPS) * g


def _norm_proj_kernel(*refs, rope, dil):
    refs = list(refs)
    x_ref, g_ref, w_ref = refs[:3]
    cos_ref, sin_ref = (refs[3], refs[4]) if rope else (None, None)
    o_ref, xn_ref = refs[5:7] if rope else refs[3:5]
    y_ref = refs[-1] if (rope or dil > 1) else None
    j = pl.program_id(1)

    @pl.when(j == 0)
    def _():
        xn_ref[...] = _rmsnorm(x_ref[...], g_ref[...]).astype(xn_ref.dtype)

    acc = jnp.dot(xn_ref[...], w_ref[...], preferred_element_type=jnp.float32)
    scale = jnp.where(j == 0, HEAD_DIM ** -0.5, 1.0).astype(jnp.float32)

    if y_ref is None:
        o_ref[...] = (acc * scale).astype(o_ref.dtype)
        return

    n_lt = acc.shape[1] // LANES
    if rope:
        @pl.when(j == 2)
        def _():
            for t in range(n_lt):
                y_ref[t] = acc[:, t * LANES:(t + 1) * LANES]

        @pl.when(j != 2)
        def _():
            cos = cos_ref[...]
            sin = sin_ref[...]
            lane = lax.broadcasted_iota(jnp.int32, cos.shape, 1)
            first_half = (lane % HEAD_DIM) < HEAD_DIM // 2
            for t in range(n_lt):
                xt = acc[:, t * LANES:(t + 1) * LANES]
                partner = jnp.where(first_half,
                                    pltpu.roll(xt, LANES - HEAD_DIM // 2, 1),
                                    pltpu.roll(xt, HEAD_DIM // 2, 1))
                y_ref[t] = (xt * cos + partner * sin) * scale
    else:
        for t in range(n_lt):
            y_ref[t] = acc[:, t * LANES:(t + 1) * LANES] * scale

    rows = acc.shape[0] // dil
    for r in range(dil):
        for t in range(n_lt):
            src = y_ref[t] if dil == 1 else y_ref[t, pl.ds(r, rows, stride=dil), :]
            c0 = r * D_MODEL + t * LANES
            o_ref[:, c0:c0 + LANES] = src.astype(o_ref.dtype)


def _norm_proj(x2, g, w, seq_len, col0=0, dil=1, rope_tables=None, tm=512):
    T, D = x2.shape
    rope = rope_tables is not None
    in_specs = [
        pl.BlockSpec((tm, D), lambda i, j: (i, 0)),
        pl.BlockSpec((1, D), lambda i, j: (0, 0)),
        pl.BlockSpec((D, D), lambda i, j: (0, col0 + j)),
    ]
    args = [x2, g, w]
    if rope:
        nper = seq_len // tm
        tab = pl.BlockSpec((tm, LANES), lambda i, j: (i % nper, 0))
        in_specs += [tab, tab]
        args += list(rope_tables)
    scratch = [pltpu.VMEM((tm, D), _MXU)]
    if rope or dil > 1:
        scratch.append(pltpu.VMEM((D // LANES, tm, LANES), jnp.float32))
    return pl.pallas_call(
        functools.partial(_norm_proj_kernel, rope=rope, dil=dil),
        grid=(T // tm, 3),
        in_specs=in_specs,
        out_specs=pl.BlockSpec((tm // dil, dil * D), lambda i, j: (i, j)),
        out_shape=jax.ShapeDtypeStruct((T // dil, 3 * dil * D), _MXU),
        scratch_shapes=scratch,
        compiler_params=_params(("parallel", "arbitrary")),
        name=f"norm_proj_rope_d{dil}" if rope else "norm_proj",
    )(*args)


def _rope_tables(seq_len):
    half = HEAD_DIM // 2
    freqs = ROPE_THETA ** (-jnp.arange(half, dtype=jnp.float32) / half)
    ang = jnp.arange(seq_len, dtype=jnp.float32)[:, None] * freqs[None, :]
    cos = jnp.cos(ang)
    sin = jnp.sin(ang)
    cos_t = jnp.tile(cos, (1, LANES // half))
    sin_t = jnp.tile(jnp.concatenate([-sin, sin], axis=1), (1, LANES // HEAD_DIM))
    return cos_t, sin_t


def _na_kernel(q_ref, k0_ref, k1_ref, k2_ref, v0_ref, v1_ref, v2_ref, bias_ref, o_ref,
               kbuf, vbuf, *, rows):
    g = pl.program_id(2)
    rp = NA_ROWS_PER_STEP
    blk = rp * GRID_W
    nblk = rows // rp
    b0 = jnp.clip(g - 1, 0, nblk - 3)
    for i, (kr, vr) in enumerate(((k0_ref, v0_ref), (k1_ref, v1_ref), (k2_ref, v2_ref))):
        kbuf[i * blk:(i + 1) * blk, :] = kr[0]
        vbuf[i * blk:(i + 1) * blk, :] = vr[0]
    nkeys = NA_KH * GRID_W
    for r in range(rp):
        r_abs = g * rp + r
        rs = jnp.clip(r_abs - NA_KH // 2, 0, rows - NA_KH)
        off = pl.multiple_of((rs - b0 * rp) * GRID_W, GRID_W)
        var = rs - r_abs + NA_KH - 1
        for h in range(LANES // HEAD_DIM):
            hs = slice(h * HEAD_DIM, (h + 1) * HEAD_DIM)
            qh = q_ref[0, r * GRID_W:(r + 1) * GRID_W, hs]
            kh = kbuf[pl.ds(off, nkeys), hs]
            vh = vbuf[pl.ds(off, nkeys), hs]
            s = lax.dot_general(qh, kh, _NT, preferred_element_type=jnp.float32)
            s = s + bias_ref[var, h]
            m = jnp.max(s, axis=-1, keepdims=True)
            e = jnp.exp(s - m)
            l = jnp.sum(e, axis=-1, keepdims=True)
            o = jnp.dot(e.astype(vh.dtype), vh, preferred_element_type=jnp.float32) / l
            o_ref[0, r * GRID_W:(r + 1) * GRID_W, hs] = o.astype(o_ref.dtype)


def _na_bias_table(rpb):
    qc = np.arange(GRID_W)[:, None]
    kc = np.arange(GRID_W)[None, :]
    cs = np.clip(qc - NA_KW // 2, 0, GRID_W - NA_KW)
    ok = (kc >= cs) & (kc < cs + NA_KW)
    col_neg = jnp.asarray(np.where(ok, 0.0, NEG_INF), jnp.float32)
    col_off = np.clip(kc - qc + NA_KW - 1, 0, 2 * NA_KW - 2)
    tab = rpb.astype(jnp.float32)[:, :, col_off] + col_neg[None, None]
    variants = []
    for var in range(NA_KH):
        t = tab[:, var:var + NA_KH]
        variants.append(jnp.transpose(t, (0, 2, 1, 3)).reshape(N_HEADS, GRID_W, NA_KH * GRID_W))
    return jnp.stack(variants, axis=0)


def _na_attention(qkv, bias_tab, B, S):
    rows = S // GRID_W
    rp = NA_ROWS_PER_STEP
    blk = rp * GRID_W
    nblk = rows // rp
    nhp = D_MODEL // LANES
    hp_heads = LANES // HEAD_DIM

    def kv_spec(which, i):
        return pl.BlockSpec(
            (1, blk, LANES),
            lambda hp, b, g: (b, jnp.clip(g - 1, 0, nblk - 3) + i, which * nhp + hp))

    return pl.pallas_call(
        functools.partial(_na_kernel, rows=rows),
        grid=(nhp, B, nblk),
        in_specs=[pl.BlockSpec((1, blk, LANES), lambda hp, b, g: (b, g, hp))]
        + [kv_spec(1, i) for i in range(3)] + [kv_spec(2, i) for i in range(3)]
        + [pl.BlockSpec((NA_KH, hp_heads, GRID_W, NA_KH * GRID_W), lambda hp, b, g: (0, hp, 0, 0))],
        out_specs=pl.BlockSpec((1, blk, LANES), lambda hp, b, g: (b, g, hp)),
        out_shape=jax.ShapeDtypeStruct((B, S, D_MODEL), qkv.dtype),
        scratch_shapes=[pltpu.VMEM((3 * blk, LANES), qkv.dtype),
                        pltpu.VMEM((3 * blk, LANES), qkv.dtype)],
        compiler_params=_params(("arbitrary", "arbitrary", "arbitrary")),
        name="na_attention",
    )(qkv, qkv, qkv, qkv, qkv, qkv, qkv, bias_tab)


def _proj_res_kernel(o_ref, w_ref, x_ref, out_ref):
    out_ref[...] = x_ref[...] + jnp.dot(o_ref[...], w_ref[...], preferred_element_type=jnp.float32)


def _proj_residual(o2, w, x2, tm=512):
    T, D = x2.shape
    return pl.pallas_call(
        _proj_res_kernel,
        grid=(T // tm,),
        in_specs=[pl.BlockSpec((tm, D), lambda i: (i, 0)),
                  pl.BlockSpec((D, D), lambda i: (0, 0)),
                  pl.BlockSpec((tm, D), lambda i: (i, 0))],
        out_specs=pl.BlockSpec((tm, D), lambda i: (i, 0)),
        out_shape=jax.ShapeDtypeStruct((T, D), jnp.float32),
        compiler_params=_params(("parallel",)),
        name="proj_residual",
    )(o2, w, x2)


def _dil_kernel(q_ref, kp_ref, kc_ref, kn_ref, vp_ref, vc_ref, vn_ref, o_ref, lse_ref, *,
                n_blocks, half):
    jb = pl.program_id(2)
    C = q_ref.shape[1]
    qi = lax.broadcasted_iota(jnp.int32, (C, C), 0)
    ci = lax.broadcasted_iota(jnp.int32, (C, C), 1)
    diff = ci - qi
    mask_c = jnp.abs(diff) <= half
    mask_p = (diff - C >= -half) & (jb > 0)
    mask_n = (diff + C <= half) & (jb < n_blocks - 1)
    lane = lax.broadcasted_iota(jnp.int32, (C, LANES), 1)
    lse_all = jnp.zeros((C, LANES), jnp.float32)
    for h in range(N_HEADS):
        hs = slice(h * HEAD_DIM, (h + 1) * HEAD_DIM)
        qh = q_ref[0, :, hs]
        sp = lax.dot_general(qh, kp_ref[0, :, hs], _NT, preferred_element_type=jnp.float32)
        sc = lax.dot_general(qh, kc_ref[0, :, hs], _NT, preferred_element_type=jnp.float32)
        sn = lax.dot_general(qh, kn_ref[0, :, hs], _NT, preferred_element_type=jnp.float32)
        sp = jnp.where(mask_p, sp, NEG_INF)
        sc = jnp.where(mask_c, sc, NEG_INF)
        sn = jnp.where(mask_n, sn, NEG_INF)
        m = jnp.maximum(jnp.maximum(jnp.max(sp, axis=-1, keepdims=True),
                                    jnp.max(sc, axis=-1, keepdims=True)),
                        jnp.max(sn, axis=-1, keepdims=True))
        ep = jnp.exp(sp - m)
        ec = jnp.exp(sc - m)
        en = jnp.exp(sn - m)
        l = (jnp.sum(ep, axis=-1, keepdims=True) + jnp.sum(ec, axis=-1, keepdims=True)
             + jnp.sum(en, axis=-1, keepdims=True))
        dt = vp_ref.dtype
        o = (jnp.dot(ep.astype(dt), vp_ref[0, :, hs], preferred_element_type=jnp.float32)
             + jnp.dot(ec.astype(dt), vc_ref[0, :, hs], preferred_element_type=jnp.float32)
             + jnp.dot(en.astype(dt), vn_ref[0, :, hs], preferred_element_type=jnp.float32))
        o_ref[0, :, hs] = (o / l).astype(o_ref.dtype)
        lse_all = jnp.where(lane == h, m + jnp.log(l), lse_all)
    lse_ref[0] = lse_all


def _dil_group(qkv_g, B, S, window, dil):
    n = S // dil
    C = DIL_BLOCK
    nb = n // C
    qkv_v = qkv_g.reshape(B, n, qkv_g.shape[-1])

    def spec(which, shift):
        def idx(b, r, jb):
            return (b, jnp.clip(jb + shift, 0, nb - 1), which * dil + r)
        return pl.BlockSpec((1, C, D_MODEL), idx)

    o, lse = pl.pallas_call(
        functools.partial(_dil_kernel, n_blocks=nb, half=window // (2 * dil)),
        grid=(B, dil, nb),
        in_specs=[spec(0, 0), spec(1, -1), spec(1, 0), spec(1, 1), spec(2, -1), spec(2, 0), spec(2, 1)],
        out_specs=[pl.BlockSpec((1, C, D_MODEL), lambda b, r, jb: (b, jb, r)),
                   pl.BlockSpec((1, C, LANES), lambda b, r, jb: (b, jb, r))],
        out_shape=[jax.ShapeDtypeStruct((B, n, dil * D_MODEL), jnp.float32),
                   jax.ShapeDtypeStruct((B, n, dil * LANES), jnp.float32)],
        compiler_params=_params(("parallel", "parallel", "arbitrary")),
        name=f"dil_group_d{dil}",
    )(qkv_v, qkv_v, qkv_v, qkv_v, qkv_v, qkv_v, qkv_v)
    return o.reshape(B * n, dil * D_MODEL), lse.reshape(B * n, dil * LANES)


def _dil_merge_kernel(o1_ref, o2_ref, o3_ref, l1_ref, l2_ref, l3_ref, x_ref, w_ref, exp_ref, out_ref,
                      o_scr, l_scr):
    tm = x_ref.shape[0]
    outs, lses = [], []
    for gi, (o_ref, l_ref) in enumerate(((o1_ref, l1_ref), (o2_ref, l2_ref), (o3_ref, l3_ref))):
        dil = DIL_CONFIGS[gi][1]
        if dil == 1:
            outs.append(o_ref[...])
            lses.append(l_ref[...])
            continue
        rows = tm // dil
        n_lt = D_MODEL // LANES
        for r in range(dil):
            for t in range(n_lt):
                c0 = r * D_MODEL + t * LANES
                o_scr[gi, t, pl.ds(r, rows, stride=dil), :] = o_ref[:, c0:c0 + LANES]
            l_scr[gi, pl.ds(r, rows, stride=dil), :] = l_ref[:, r * LANES:(r + 1) * LANES]
        outs.append(jnp.concatenate([o_scr[gi, t] for t in range(n_lt)], axis=1))
        lses.append(l_scr[gi])
    m = jnp.maximum(jnp.maximum(lses[0], lses[1]), lses[2])
    es = [jnp.exp(l - m) for l in lses]
    z = es[0] + es[1] + es[2]
    merged = None
    for e, o in zip(es, outs):
        w = e / z
        hi = w.astype(_MXU)
        lo = (w - hi.astype(jnp.float32)).astype(_MXU)
        wide = (jnp.dot(hi, exp_ref[...], preferred_element_type=jnp.float32)
                + jnp.dot(lo, exp_ref[...], preferred_element_type=jnp.float32))
        term = wide * o
        merged = term if merged is None else merged + term
    out_ref[...] = x_ref[...] + jnp.dot(merged.astype(_MXU), w_ref[...],
                                        preferred_element_type=jnp.float32)


def _dil_merge_proj(outs, lses, x2, w, tm=256):
    T, D = x2.shape
    expand = np.zeros((LANES, D), np.float32)
    for h in range(N_HEADS):
        expand[h, h * HEAD_DIM:(h + 1) * HEAD_DIM] = 1.0
    expand = jnp.asarray(expand, _MXU)
    row = pl.BlockSpec((tm, D), lambda i: (i, 0))
    o_specs = [pl.BlockSpec((tm // d, d * D), lambda i: (i, 0)) for _, d in DIL_CONFIGS]
    l_specs = [pl.BlockSpec((tm // d, d * LANES), lambda i: (i, 0)) for _, d in DIL_CONFIGS]
    ng = len(DIL_CONFIGS)
    return pl.pallas_call(
        _dil_merge_kernel,
        grid=(T // tm,),
        in_specs=o_specs + l_specs + [row, pl.BlockSpec((D, D), lambda i: (0, 0)),
                                      pl.BlockSpec((LANES, D), lambda i: (0, 0))],
        out_specs=row,
        out_shape=jax.ShapeDtypeStruct((T, D), jnp.float32),
        scratch_shapes=[pltpu.VMEM((ng, D // LANES, tm, LANES), jnp.float32),
                        pltpu.VMEM((ng, tm, LANES), jnp.float32)],
        compiler_params=_params(("parallel",)),
        name="dil_merge_proj",
    )(*outs, *lses, x2, w, expand)


def _top16_rows(s, want_rank):
    K, L = s.shape
    kio = lax.broadcasted_iota(jnp.int32, (K, L), 0).astype(jnp.float32)
    aio = lax.broadcasted_iota(jnp.int32, (PEER_TOPK, L), 0)
    vals = jnp.zeros((PEER_TOPK, L), jnp.float32)
    idxs = jnp.zeros((PEER_TOPK, L), jnp.float32)
    rank = jnp.full((K, L), float(PEER_TOPK), jnp.float32) if want_rank else None
    for a in range(PEER_TOPK):
        m = jnp.max(s, axis=0, keepdims=True)
        idx = jnp.min(jnp.where(s == m, kio, float(K)), axis=0, keepdims=True)
        sel = kio == idx
        if want_rank:
            rank = jnp.where(sel, float(a), rank)
        s = jnp.where(sel, -jnp.inf, s)
        vals = jnp.where(aio == a, m, vals)
        idxs = jnp.where(aio == a, idx, idxs)
    return vals, idxs, rank


def _pair_counts(v1, v2):
    L = v1.shape[1]
    K = PEER_TOPK
    blocks = [v1 + v2[0:1]]
    for b in range(1, 8):
        blocks.append(v1[0:8] + v2[b:b + 1])
    blocks.append(v1[0:1] + v2[8:16])
    cand = jnp.concatenate(blocks, axis=0)
    R = cand.shape[0]
    r = lax.broadcasted_iota(jnp.int32, (R, L), 0)
    a_mid = (r - 16) & 7
    b_mid = ((r - 16) >> 3) + 1
    a_of = jnp.where(r < 16, r, jnp.where(r < 72, a_mid, 0))
    b_of = jnp.where(r < 16, 0, jnp.where(r < 72, b_mid, r - 64))
    flat = (a_of * K + b_of).astype(jnp.float32)
    a_of_f = a_of.astype(jnp.float32)
    cand = jnp.where((a_of + 1) * (b_of + 1) <= K, cand, -jnp.inf)
    aio = lax.broadcasted_iota(jnp.int32, (K, L), 0).astype(jnp.float32)
    cnt = jnp.zeros((K, L), jnp.float32)
    top = v1[0:1] + v2[0:1]
    z = jnp.zeros((1, L), jnp.float32)
    for _ in range(K):
        m = jnp.max(cand, axis=0, keepdims=True)
        f = jnp.min(jnp.where(cand == m, flat, float(K * K)), axis=0, keepdims=True)
        sel = flat == f
        a_sel = jnp.max(jnp.where(sel, a_of_f, 0.0), axis=0, keepdims=True)
        cand = jnp.where(sel, -jnp.inf, cand)
        cnt = cnt + jnp.where(aio == a_sel, 1.0, 0.0)
        z = z + jnp.exp(m - top)
    return cnt, z


def _route_head(s1, s2):
    v1, i1, _ = _top16_rows(s1, False)
    v2, _, rank2 = _top16_rows(s2, True)
    cnt, z = _pair_counts(v1, v2)
    kio = lax.broadcasted_iota(jnp.int32, s1.shape, 0).astype(jnp.float32)
    cnt_i = jnp.zeros_like(s1)
    for a in range(PEER_TOPK):
        cnt_i = jnp.where(kio == i1[a:a + 1], cnt[a:a + 1], cnt_i)
    e1 = jnp.exp(s1 - v1[0:1])
    e2 = jnp.exp(s2 - v2[0:1]) / z
    return cnt_i, e1, rank2, e2


def _router_kernel(x_ref, gb_ref, wqt_ref, k1h_ref, k1l_ref, k2h_ref, k2l_ref,
                   hnt_ref, row_ref, rank2_ref, e2_ref):
    xt = x_ref[...].T
    ms = jnp.mean(xt * xt, axis=0, keepdims=True)
    hnt = (xt * lax.rsqrt(ms + RMS_EPS) * gb_ref[...]).astype(hnt_ref.dtype)
    hnt_ref[...] = hnt
    qt = jnp.dot(wqt_ref[...], hnt, preferred_element_type=jnp.float32)
    q_hi = qt.astype(_MXU)
    q_lo = (qt - q_hi.astype(jnp.float32)).astype(_MXU)
    L = xt.shape[1]

    def scores(kh_ref, kl_ref, h, r0, ls):
        qh = q_hi[r0:r0 + N_KEYS, ls]
        ql = q_lo[r0:r0 + N_KEYS, ls]
        return (jnp.dot(kh_ref[h], qh, preferred_element_type=jnp.float32)
                + jnp.dot(kh_ref[h], ql, preferred_element_type=jnp.float32)
                + jnp.dot(kl_ref[h], qh, preferred_element_type=jnp.float32))

    for h in range(PEER_HEADS):
        for lt in range(L // LANES):
            ls = slice(lt * LANES, (lt + 1) * LANES)
            s1 = scores(k1h_ref, k1l_ref, h, h * 2 * N_KEYS, ls)
            s2 = scores(k2h_ref, k2l_ref, h, h * 2 * N_KEYS + N_KEYS, ls)
            cnt_i, e1, rank2, e2 = _route_head(s1, s2)
            row_ref[2 * h, :, ls] = cnt_i
            row_ref[2 * h + 1, :, ls] = e1
            rank2_ref[h, :, ls] = rank2.astype(rank2_ref.dtype)
            e2_ref[h, :, ls] = e2.astype(e2_ref.dtype)


def _peer_router(x2, g, wqt, k1h, k1l, k2h, k2l, tm=256):
    T, D = x2.shape
    NQ = wqt.shape[0]
    H = PEER_HEADS
    gb = jnp.broadcast_to(g.reshape(D, 1), (D, tm))
    kspec = pl.BlockSpec((H, N_KEYS, N_KEYS), lambda i: (0, 0, 0))
    return pl.pallas_call(
        _router_kernel,
        grid=(T // tm,),
        in_specs=[pl.BlockSpec((tm, D), lambda i: (i, 0)),
                  pl.BlockSpec((D, tm), lambda i: (0, 0)),
                  pl.BlockSpec((NQ, D), lambda i: (0, 0)),
                  kspec, kspec, kspec, kspec],
        out_specs=[pl.BlockSpec((D, tm), lambda i: (0, i)),
                   pl.BlockSpec((2 * H, N_KEYS, tm), lambda i: (0, 0, i)),
                   pl.BlockSpec((H, N_KEYS, tm), lambda i: (0, 0, i)),
                   pl.BlockSpec((H, N_KEYS, tm), lambda i: (0, 0, i))],
        out_shape=[jax.ShapeDtypeStruct((D, T), _MXU),
                   jax.ShapeDtypeStruct((2 * H, N_KEYS, T), jnp.float32),
                   jax.ShapeDtypeStruct((H, N_KEYS, T), _GATE),
                   jax.ShapeDtypeStruct((H, N_KEYS, T), _GATE)],
        compiler_params=_params(("parallel",)),
        name="peer_router",
    )(x2, gb, wqt, k1h, k1l, k2h, k2l)


def _gelu(a):
    return 0.5 * a * (1.0 + lax.erf(a * (2.0 ** -0.5)))


def _peer_dense_kernel(*refs, n_e, final_norm):
    refs = list(refs)
    hnt_ref, u_ref, vt_ref, row_ref, rank2_ref, e2_ref, x_ref = refs[:7]
    gf_ref = refs[7] if final_norm else None
    out_ref, acc_ref, at_ref, gt_ref = refs[-4:]
    s = pl.program_id(0)
    n_pairs = pl.num_programs(0) - 2
    eb = u_ref.shape[0]
    sub = eb // N_KEYS

    @pl.when(s == 0)
    def _():
        at_ref[...] = jnp.zeros_like(at_ref)
        gt_ref[...] = jnp.zeros_like(gt_ref)

    slot = s % 2
    e_g = jnp.clip(s - 1, 0, n_pairs - 1) % n_e
    e_y = jnp.clip(s - 2, 0, n_pairs - 1) % n_e

    at_ref[slot] = jnp.dot(u_ref[...], hnt_ref[...], preferred_element_type=jnp.float32)

    for ii in range(sub):
        i = e_g * sub + ii
        rs = slice(ii * N_KEYS, (ii + 1) * N_KEYS)
        act = _gelu(at_ref[1 - slot, rs, :])
        gate = None
        for h in range(PEER_HEADS):
            cnt = row_ref[2 * h, pl.ds(i, 1), :].astype(_GATE)
            e1 = row_ref[2 * h + 1, pl.ds(i, 1), :].astype(_GATE)
            term = jnp.where(rank2_ref[h] < cnt, e2_ref[h], jnp.zeros((), _GATE)) * e1
            gate = term if gate is None else gate + term
        gt_ref[1 - slot, rs, :] = (gate * act.astype(_GATE)).astype(gt_ref.dtype)

    y = jnp.dot(vt_ref[...], gt_ref[slot], preferred_element_type=jnp.float32)
    acc_ref[...] = jnp.where(e_y == 0, y, acc_ref[...] + y)

    @pl.when((e_y == n_e - 1) & (s >= 2))
    def _():
        res = x_ref[...] + acc_ref[...].T
        if final_norm:
            res = _rmsnorm(res, gf_ref[...])
        out_ref[...] = res


def _peer_dense(hnt, u, vt, rowdat, rank2, e2, x2, g_final=None, tm=512, eb=1024):
    T, D = x2.shape
    H = PEER_HEADS
    n_e = N_EXPERTS // eb
    n_pairs = (T // tm) * n_e
    final_norm = g_final is not None

    def pair(s, lag):
        p = jnp.clip(s - lag, 0, n_pairs - 1)
        return p // n_e, p % n_e

    in_specs = [pl.BlockSpec((D, tm), lambda s: (0, pair(s, 0)[0])),
                pl.BlockSpec((eb, D), lambda s: (pair(s, 0)[1], 0)),
                pl.BlockSpec((D, eb), lambda s: (0, pair(s, 2)[1])),
                pl.BlockSpec((2 * H, N_KEYS, tm), lambda s: (0, 0, pair(s, 1)[0])),
                pl.BlockSpec((H, N_KEYS, tm), lambda s: (0, 0, pair(s, 1)[0])),
                pl.BlockSpec((H, N_KEYS, tm), lambda s: (0, 0, pair(s, 1)[0])),
                pl.BlockSpec((tm, D), lambda s: (pair(s, 2)[0], 0))]
    args = [hnt, u, vt, rowdat, rank2, e2, x2]
    if final_norm:
        in_specs.append(pl.BlockSpec((1, D), lambda s: (0, 0)))
        args.append(g_final)
    return pl.pallas_call(
        functools.partial(_peer_dense_kernel, n_e=n_e, final_norm=final_norm),
        grid=(n_pairs + 2,),
        in_specs=in_specs,
        out_specs=pl.BlockSpec((tm, D), lambda s: (pair(s, 2)[0], 0)),
        out_shape=jax.ShapeDtypeStruct((T, D), jnp.float32),
        scratch_shapes=[pltpu.VMEM((D, tm), jnp.float32),
                        pltpu.VMEM((2, eb, tm), jnp.float32),
                        pltpu.VMEM((2, eb, tm), _MXU)],
        compiler_params=_params(("arbitrary",)),
        name="peer_dense_final" if final_norm else "peer_dense",
    )(*args)


def _split_hi_lo(w):
    hi = w.astype(_MXU)
    lo = (w - hi.astype(jnp.float32)).astype(_MXU)
    return hi, lo


def _peer(x2, g, wqt, sk1, sk2, u, vt, g_final=None):
    k1h, k1l = _split_hi_lo(sk1)
    k2h, k2l = _split_hi_lo(sk2)
    hnt, rowdat, rank2, e2 = _peer_router(x2, g, wqt, k1h, k1l, k2h, k2l)
    return _peer_dense(hnt, u, vt, rowdat, rank2, e2, x2, g_final)


def _trunk(x, p):
    B, S, D = x.shape
    x2 = x.reshape(B * S, D)
    qkv = _norm_proj(x2, p["ln_mix"][0:1], p["w_qkv_na"], S)
    o = _na_attention(qkv.reshape(B, S, 3 * D), p["na_bias"], B, S)
    x2 = _proj_residual(o.reshape(B * S, D), p["w_o_na"], x2)
    x2 = _peer(x2, p["ln_ffn"][0:1], p["wqt"][0], p["sk1"][0], p["sk2"][0], p["u"][0], p["vt"][0])
    tables = _rope_tables(S)
    outs, lses = [], []
    for gi, (window, dil) in enumerate(DIL_CONFIGS):
        qkv_g = _norm_proj(x2, p["ln_mix"][1:2], p["w_qkv_dil"], S, col0=3 * gi, dil=dil, rope_tables=tables)
        o_g, lse_g = _dil_group(qkv_g, B, S, window, dil)
        outs.append(o_g)
        lses.append(lse_g)
    x2 = _dil_merge_proj(outs, lses, x2, p["w_o_dil"])
    x2 = _peer(x2, p["ln_ffn"][1:2], p["wqt"][1], p["sk1"][1], p["sk2"][1], p["u"][1], p["vt"][1],
               g_final=p["ln_final"])
    return x2.reshape(B, S, D)


def kernel(x_prompt, x_sample, ln_mix, ln_ffn, ln_final, w_qkv_na, rpb_na, w_o_na, w_qkv_dil, w_o_dil,
           w_query_peer, subkeys1_peer, subkeys2_peer, u_peer, v_peer):
    p = {
        "ln_mix": ln_mix, "ln_ffn": ln_ffn, "ln_final": ln_final.reshape(1, -1),
        "w_qkv_na": w_qkv_na[0].astype(_MXU), "w_o_na": w_o_na[0].astype(_MXU),
        "na_bias": _na_bias_table(rpb_na[0]),
        "w_qkv_dil": w_qkv_dil[0].astype(_MXU), "w_o_dil": w_o_dil[0].astype(_MXU),
        "wqt": jnp.swapaxes(w_query_peer, 1, 2).astype(_MXU),
        "sk1": subkeys1_peer, "sk2": subkeys2_peer,
        "u": u_peer.astype(_MXU), "vt": jnp.swapaxes(v_peer, 1, 2).astype(_MXU),
    }
    return _trunk(x_prompt, p), _trunk(x_sample, p)
```

```python
import functools

import numpy as np
import jax
import jax.numpy as jnp
from jax import lax
from jax.experimental import pallas as pl
from jax.experimental.pallas import tpu as pltpu

D_MODEL = 1024
HEAD_DIM = 64
N_HEADS = 16
GRID_W = 64
NA_KH = 8
NA_KW = 16
NA_ROWS_PER_STEP = 4
NA_HEAD_LANES = 512
DIL_CONFIGS = ((128, 1), (512, 4), (2048, 16))
DIL_BLOCK = 128
ROPE_THETA = 10000.0
PEER_HEADS = 8
N_KEYS = 128
N_EXPERTS = N_KEYS * N_KEYS
PEER_TOPK = 16
RMS_EPS = 1e-6
NEG_INF = -1e30
LANES = 128

_MXU = jnp.bfloat16
_GATE = jnp.bfloat16
_VMEM_LIMIT = 48 * 1024 * 1024

_NT = (((1,), (1,)), ((), ()))

_MARK = -(2.0 ** 127)
_MARK_STEP = 2.0 ** 104
_MARK_LIMIT = -(2.0 ** 126)
_SCORE_FLOOR = -(2.0 ** 120)


def _params(sem):
    return pltpu.CompilerParams(dimension_semantics=sem, vmem_limit_bytes=_VMEM_LIMIT)


def _rmsnorm(x, g):
    ms = jnp.mean(x * x, axis=-1, keepdims=True)
    return x * lax.rsqrt(ms + RMS_EPS) * g


def _norm_proj_kernel(*refs, rope, dil):
    refs = list(refs)
    x_ref, g_ref, w_ref = refs[:3]
    cos_ref, sin_ref = (refs[3], refs[4]) if rope else (None, None)
    o_ref, xn_ref = refs[5:7] if rope else refs[3:5]
    y_ref = refs[-1] if (rope or dil > 1) else None
    j = pl.program_id(1)

    @pl.when(j == 0)
    def _():
        xn_ref[...] = _rmsnorm(x_ref[...], g_ref[...]).astype(xn_ref.dtype)

    acc = jnp.dot(xn_ref[...], w_ref[...], preferred_element_type=jnp.float32)
    scale = jnp.where(j == 0, HEAD_DIM ** -0.5, 1.0).astype(jnp.float32)

    if y_ref is None:
        o_ref[...] = (acc * scale).astype(o_ref.dtype)
        return

    n_lt = acc.shape[1] // LANES
    if rope:
        @pl.when(j == 2)
        def _():
            for t in range(n_lt):
                y_ref[t] = acc[:, t * LANES:(t + 1) * LANES]

        @pl.when(j != 2)
        def _():
            cos = cos_ref[...]
            sin = sin_ref[...]
            lane = lax.broadcasted_iota(jnp.int32, cos.shape, 1)
            first_half = (lane % HEAD_DIM) < HEAD_DIM // 2
            for t in range(n_lt):
                xt = acc[:, t * LANES:(t + 1) * LANES]
                partner = jnp.where(first_half,
                                    pltpu.roll(xt, LANES - HEAD_DIM // 2, 1),
                                    pltpu.roll(xt, HEAD_DIM // 2, 1))
                y_ref[t] = (xt * cos + partner * sin) * scale
    else:
        for t in range(n_lt):
            y_ref[t] = acc[:, t * LANES:(t + 1) * LANES] * scale

    rows = acc.shape[0] // dil
    for r in range(dil):
        for t in range(n_lt):
            src = y_ref[t] if dil == 1 else y_ref[t, pl.ds(r, rows, stride=dil), :]
            c0 = r * D_MODEL + t * LANES
            o_ref[:, c0:c0 + LANES] = src.astype(o_ref.dtype)


def _norm_proj(x2, g, w, seq_len, col0=0, dil=1, rope_tables=None, tm=512):
    T, D = x2.shape
    rope = rope_tables is not None
    in_specs = [
        pl.BlockSpec((tm, D), lambda i, j: (i, 0)),
        pl.BlockSpec((1, D), lambda i, j: (0, 0)),
        pl.BlockSpec((D, D), lambda i, j: (0, col0 + j)),
    ]
    args = [x2, g, w]
    if rope:
        nper = seq_len // tm
        tab = pl.BlockSpec((tm, LANES), lambda i, j: (i % nper, 0))
        in_specs += [tab, tab]
        args += list(rope_tables)
    scratch = [pltpu.VMEM((tm, D), _MXU)]
    if rope or dil > 1:
        scratch.append(pltpu.VMEM((D // LANES, tm, LANES), jnp.float32))
    return pl.pallas_call(
        functools.partial(_norm_proj_kernel, rope=rope, dil=dil),
        grid=(T // tm, 3),
        in_specs=in_specs,
        out_specs=pl.BlockSpec((tm // dil, dil * D), lambda i, j: (i, j)),
        out_shape=jax.ShapeDtypeStruct((T // dil, 3 * dil * D), _MXU),
        scratch_shapes=scratch,
        compiler_params=_params(("parallel", "arbitrary")),
        name=f"norm_proj_rope_d{dil}" if rope else "norm_proj",
    )(*args)


def _rope_tables(seq_len):
    half = HEAD_DIM // 2
    freqs = ROPE_THETA ** (-jnp.arange(half, dtype=jnp.float32) / half)
    ang = jnp.arange(seq_len, dtype=jnp.float32)[:, None] * freqs[None, :]
    cos = jnp.cos(ang)
    sin = jnp.sin(ang)
    cos_t = jnp.tile(cos, (1, LANES // half))
    sin_t = jnp.tile(jnp.concatenate([-sin, sin], axis=1), (1, LANES // HEAD_DIM))
    return cos_t, sin_t


def _na_kernel(q_ref, k0_ref, k1_ref, k2_ref, v0_ref, v1_ref, v2_ref, bias_ref, o_ref,
               kbuf, vbuf, s_scr, m_scr, e_scr, *, rows):
    g = pl.program_id(2)
    rp = NA_ROWS_PER_STEP
    blk = rp * GRID_W
    nblk = rows // rp
    b0 = jnp.clip(g - 1, 0, nblk - 3)
    for i, (kr, vr) in enumerate(((k0_ref, v0_ref), (k1_ref, v1_ref), (k2_ref, v2_ref))):
        kbuf[i * blk:(i + 1) * blk, :] = kr[0]
        vbuf[i * blk:(i + 1) * blk, :] = vr[0]
    nkeys = NA_KH * GRID_W
    ones = jnp.ones((nkeys, LANES), kbuf.dtype)
    nh = q_ref.shape[2] // HEAD_DIM
    offs = []
    for r in range(rp):
        r_abs = g * rp + r
        rs = jnp.clip(r_abs - NA_KH // 2, 0, rows - NA_KH)
        off = pl.multiple_of((rs - b0 * rp) * GRID_W, GRID_W)
        offs.append(off)
        var = rs - r_abs + NA_KH - 1
        for h in range(nh):
            hs = slice(h * HEAD_DIM, (h + 1) * HEAD_DIM)
            qh = q_ref[0, r * GRID_W:(r + 1) * GRID_W, hs]
            s = lax.dot_general(qh, kbuf[pl.ds(off, nkeys), hs], _NT, preferred_element_type=jnp.float32)
            s = s + bias_ref[var, h]
            s_scr[r * nh + h] = s
            m_scr[r * nh + h] = jnp.broadcast_to(jnp.max(s, axis=-1, keepdims=True), (GRID_W, LANES))
    for u in range(rp * nh):
        mb = m_scr[u]
        e_scr[u] = jnp.exp(s_scr[u] - jnp.concatenate([mb] * (nkeys // LANES), axis=1)).astype(e_scr.dtype)
    for r in range(rp):
        for h in range(nh):
            hs = slice(h * HEAD_DIM, (h + 1) * HEAD_DIM)
            e = e_scr[r * nh + h]
            l = jnp.dot(e, ones, preferred_element_type=jnp.float32)
            o = jnp.dot(e, vbuf[pl.ds(offs[r], nkeys), hs], preferred_element_type=jnp.float32)
            o_ref[0, r * GRID_W:(r + 1) * GRID_W, hs] = (o / l[:, :HEAD_DIM]).astype(o_ref.dtype)


def _na_bias_table(rpb):
    qc = np.arange(GRID_W)[:, None]
    kc = np.arange(GRID_W)[None, :]
    cs = np.clip(qc - NA_KW // 2, 0, GRID_W - NA_KW)
    ok = (kc >= cs) & (kc < cs + NA_KW)
    col_neg = jnp.asarray(np.where(ok, 0.0, NEG_INF), jnp.float32)
    col_off = np.clip(kc - qc + NA_KW - 1, 0, 2 * NA_KW - 2)
    tab = rpb.astype(jnp.float32)[:, :, col_off] + col_neg[None, None]
    variants = []
    for var in range(NA_KH):
        t = tab[:, var:var + NA_KH]
        variants.append(jnp.transpose(t, (0, 2, 1, 3)).reshape(N_HEADS, GRID_W, NA_KH * GRID_W))
    return jnp.stack(variants, axis=0)


def _na_attention(qkv, bias_tab, B, S):
    rows = S // GRID_W
    rp = NA_ROWS_PER_STEP
    blk = rp * GRID_W
    nblk = rows // rp
    wl = NA_HEAD_LANES
    nhp = D_MODEL // wl
    hp_heads = wl // HEAD_DIM

    def kv_spec(which, i):
        return pl.BlockSpec(
            (1, blk, wl),
            lambda hp, b, g: (b, jnp.clip(g - 1, 0, nblk - 3) + i, which * nhp + hp))

    return pl.pallas_call(
        functools.partial(_na_kernel, rows=rows),
        grid=(nhp, B, nblk),
        in_specs=[pl.BlockSpec((1, blk, wl), lambda hp, b, g: (b, g, hp))]
        + [kv_spec(1, i) for i in range(3)] + [kv_spec(2, i) for i in range(3)]
        + [pl.BlockSpec((NA_KH, hp_heads, GRID_W, NA_KH * GRID_W), lambda hp, b, g: (0, hp, 0, 0))],
        out_specs=pl.BlockSpec((1, blk, wl), lambda hp, b, g: (b, g, hp)),
        out_shape=jax.ShapeDtypeStruct((B, S, D_MODEL), qkv.dtype),
        scratch_shapes=[pltpu.VMEM((3 * blk, wl), qkv.dtype),
                        pltpu.VMEM((3 * blk, wl), qkv.dtype),
                        pltpu.VMEM((rp * hp_heads, GRID_W, NA_KH * GRID_W), jnp.float32),
                        pltpu.VMEM((rp * hp_heads, GRID_W, LANES), jnp.float32),
                        pltpu.VMEM((rp * hp_heads, GRID_W, NA_KH * GRID_W), qkv.dtype)],
        compiler_params=_params(("arbitrary", "arbitrary", "arbitrary")),
        name="na_attention",
    )(qkv, qkv, qkv, qkv, qkv, qkv, qkv, bias_tab)


def _proj_res_kernel(o_ref, w_ref, x_ref, out_ref):
    out_ref[...] = x_ref[...] + jnp.dot(o_ref[...], w_ref[...], preferred_element_type=jnp.float32)


def _proj_residual(o2, w, x2, tm=512):
    T, D = x2.shape
    return pl.pallas_call(
        _proj_res_kernel,
        grid=(T // tm,),
        in_specs=[pl.BlockSpec((tm, D), lambda i: (i, 0)),
                  pl.BlockSpec((D, D), lambda i: (0, 0)),
                  pl.BlockSpec((tm, D), lambda i: (i, 0))],
        out_specs=pl.BlockSpec((tm, D), lambda i: (i, 0)),
        out_shape=jax.ShapeDtypeStruct((T, D), jnp.float32),
        compiler_params=_params(("parallel",)),
        name="proj_residual",
    )(o2, w, x2)


def _dil_kernel(q_ref, kp_ref, kc_ref, kn_ref, vp_ref, vc_ref, vn_ref, o_ref, lse_ref, kbuf, vbuf,
                s_scr, m_scr, e_scr, *,
                n_blocks, half):
    jb = pl.program_id(2)
    C = q_ref.shape[1]
    hc = C // 2
    nk = 2 * C
    for buf, (p_ref, c_ref, n_ref) in ((kbuf, (kp_ref, kc_ref, kn_ref)), (vbuf, (vp_ref, vc_ref, vn_ref))):
        buf[0:hc, :] = p_ref[0]
        buf[hc:hc + C, :] = c_ref[0]
        buf[hc + C:nk, :] = n_ref[0]
    qi = lax.broadcasted_iota(jnp.int32, (C, nk), 0)
    ci = lax.broadcasted_iota(jnp.int32, (C, nk), 1)
    lo = jnp.where(jb == 0, hc, 0)
    hi = jnp.where(jb == n_blocks - 1, hc + C, nk)
    mask = (jnp.abs(ci - hc - qi) <= half) & (ci >= lo) & (ci < hi)
    ones = jnp.ones((nk, LANES), kbuf.dtype)
    lane = lax.broadcasted_iota(jnp.int32, (C, LANES), 1)
    lse_all = jnp.zeros((C, LANES), jnp.float32)
    for h in range(N_HEADS):
        hs = slice(h * HEAD_DIM, (h + 1) * HEAD_DIM)
        s = lax.dot_general(q_ref[0, :, hs], kbuf[:, hs], _NT, preferred_element_type=jnp.float32)
        s = jnp.where(mask, s, NEG_INF)
        s_scr[h] = s
        m = jnp.max(jnp.maximum(s[:, :C], s[:, C:]), axis=-1, keepdims=True)
        m_scr[h] = jnp.broadcast_to(m, (C, LANES))
    for h in range(N_HEADS):
        mb = m_scr[h]
        e_scr[h] = jnp.exp(s_scr[h] - jnp.concatenate([mb, mb], axis=1)).astype(e_scr.dtype)
    for h in range(N_HEADS):
        hs = slice(h * HEAD_DIM, (h + 1) * HEAD_DIM)
        e = e_scr[h]
        l = jnp.dot(e, ones, preferred_element_type=jnp.float32)
        o = jnp.dot(e, vbuf[:, hs], preferred_element_type=jnp.float32)
        o_ref[0, :, hs] = (o / l[:, :HEAD_DIM]).astype(o_ref.dtype)
        lse_all = jnp.where(lane == h, m_scr[h] + jnp.log(l), lse_all)
    lse_ref[0] = lse_all


def _dil_group(qkv_g, B, S, window, dil):
    n = S // dil
    C = DIL_BLOCK
    hc = C // 2
    nb = n // C
    half = window // (2 * dil)
    assert half <= hc and n % C == 0
    qkv_v = qkv_g.reshape(B, n, qkv_g.shape[-1])

    def cur(which):
        return pl.BlockSpec((1, C, D_MODEL), lambda b, r, jb: (b, jb, which * dil + r))

    def edge(which, shift):
        def idx(b, r, jb):
            return (b, jnp.clip(2 * jb + shift, 0, 2 * nb - 1), which * dil + r)
        return pl.BlockSpec((1, hc, D_MODEL), idx)

    o, lse = pl.pallas_call(
        functools.partial(_dil_kernel, n_blocks=nb, half=half),
        grid=(B, dil, nb),
        in_specs=[cur(0), edge(1, -1), cur(1), edge(1, 2), edge(2, -1), cur(2), edge(2, 2)],
        out_specs=[pl.BlockSpec((1, C, D_MODEL), lambda b, r, jb: (b, jb, r)),
                   pl.BlockSpec((1, C, LANES), lambda b, r, jb: (b, jb, r))],
        out_shape=[jax.ShapeDtypeStruct((B, n, dil * D_MODEL), jnp.float32),
                   jax.ShapeDtypeStruct((B, n, dil * LANES), jnp.float32)],
        scratch_shapes=[pltpu.VMEM((2 * C, D_MODEL), qkv_g.dtype), pltpu.VMEM((2 * C, D_MODEL), qkv_g.dtype),
                        pltpu.VMEM((N_HEADS, C, 2 * C), jnp.float32),
                        pltpu.VMEM((N_HEADS, C, LANES), jnp.float32),
                        pltpu.VMEM((N_HEADS, C, 2 * C), qkv_g.dtype)],
        compiler_params=_params(("parallel", "parallel", "arbitrary")),
        name=f"dil_group_d{dil}",
    )(qkv_v, qkv_v, qkv_v, qkv_v, qkv_v, qkv_v, qkv_v)
    return o.reshape(B * n, dil * D_MODEL), lse.reshape(B * n, dil * LANES)


def _dil_merge_kernel(o1_ref, o2_ref, o3_ref, l1_ref, l2_ref, l3_ref, x_ref, w_ref, exp_ref, out_ref,
                      o_scr, l_scr):
    tm = x_ref.shape[0]
    outs, lses = [], []
    for gi, (o_ref, l_ref) in enumerate(((o1_ref, l1_ref), (o2_ref, l2_ref), (o3_ref, l3_ref))):
        dil = DIL_CONFIGS[gi][1]
        if dil == 1:
            outs.append(o_ref[...])
            lses.append(l_ref[...])
            continue
        rows = tm // dil
        n_lt = D_MODEL // LANES
        for r in range(dil):
            for t in range(n_lt):
                c0 = r * D_MODEL + t * LANES
                o_scr[gi, t, pl.ds(r, rows, stride=dil), :] = o_ref[:, c0:c0 + LANES]
            l_scr[gi, pl.ds(r, rows, stride=dil), :] = l_ref[:, r * LANES:(r + 1) * LANES]
        outs.append(jnp.concatenate([o_scr[gi, t] for t in range(n_lt)], axis=1))
        lses.append(l_scr[gi])
    m = jnp.maximum(jnp.maximum(lses[0], lses[1]), lses[2])
    es = [jnp.exp(l - m) for l in lses]
    z = es[0] + es[1] + es[2]
    merged = None
    for e, o in zip(es, outs):
        w = e / z
        hi = w.astype(_MXU)
        lo = (w - hi.astype(jnp.float32)).astype(_MXU)
        wide = (jnp.dot(hi, exp_ref[...], preferred_element_type=jnp.float32)
                + jnp.dot(lo, exp_ref[...], preferred_element_type=jnp.float32))
        term = wide * o
        merged = term if merged is None else merged + term
    out_ref[...] = x_ref[...] + jnp.dot(merged.astype(_MXU), w_ref[...],
                                        preferred_element_type=jnp.float32)


def _dil_merge_proj(outs, lses, x2, w, tm=256):
    T, D = x2.shape
    expand = np.zeros((LANES, D), np.float32)
    for h in range(N_HEADS):
        expand[h, h * HEAD_DIM:(h + 1) * HEAD_DIM] = 1.0
    expand = jnp.asarray(expand, _MXU)
    row = pl.BlockSpec((tm, D), lambda i: (i, 0))
    o_specs = [pl.BlockSpec((tm // d, d * D), lambda i: (i, 0)) for _, d in DIL_CONFIGS]
    l_specs = [pl.BlockSpec((tm // d, d * LANES), lambda i: (i, 0)) for _, d in DIL_CONFIGS]
    ng = len(DIL_CONFIGS)
    return pl.pallas_call(
        _dil_merge_kernel,
        grid=(T // tm,),
        in_specs=o_specs + l_specs + [row, pl.BlockSpec((D, D), lambda i: (0, 0)),
                                      pl.BlockSpec((LANES, D), lambda i: (0, 0))],
        out_specs=row,
        out_shape=jax.ShapeDtypeStruct((T, D), jnp.float32),
        scratch_shapes=[pltpu.VMEM((ng, D // LANES, tm, LANES), jnp.float32),
                        pltpu.VMEM((ng, tm, LANES), jnp.float32)],
        compiler_params=_params(("parallel",)),
        name="dil_merge_proj",
    )(*outs, *lses, x2, w, expand)


def _top16_rows(s, want_rank):
    K, L = s.shape
    kio = lax.broadcasted_iota(jnp.int32, (K, L), 0).astype(jnp.float32)
    aio = lax.broadcasted_iota(jnp.int32, (PEER_TOPK, L), 0)
    vals = jnp.zeros((PEER_TOPK, L), jnp.float32)
    idxs = jnp.zeros((PEER_TOPK, L), jnp.float32)
    rank = jnp.full((K, L), float(PEER_TOPK), jnp.float32) if want_rank else None
    for a in range(PEER_TOPK):
        m = jnp.max(s, axis=0, keepdims=True)
        idx = jnp.min(jnp.where(s == m, kio, float(K)), axis=0, keepdims=True)
        sel = kio == idx
        if want_rank:
            rank = jnp.where(sel, float(a), rank)
        s = jnp.where(sel, -jnp.inf, s)
        vals = jnp.where(aio == a, m, vals)
        idxs = jnp.where(aio == a, idx, idxs)
    return vals, idxs, rank


def _pair_candidates(v1, v2):
    L = v1.shape[1]
    K = PEER_TOPK
    blocks = [v1 + v2[0:1]]
    for b in range(1, 8):
        blocks.append(v1[0:8] + v2[b:b + 1])
    blocks.append(v1[0:1] + v2[8:16])
    cand = jnp.concatenate(blocks, axis=0)
    R = cand.shape[0]
    r = lax.broadcasted_iota(jnp.int32, (R, L), 0)
    a_mid = (r - 16) & 7
    b_mid = ((r - 16) >> 3) + 1
    a_of = jnp.where(r < 16, r, jnp.where(r < 72, a_mid, 0))
    b_of = jnp.where(r < 16, 0, jnp.where(r < 72, b_mid, r - 64))
    flat = (a_of * K + b_of).astype(jnp.float32)
    cand = jnp.where((a_of + 1) * (b_of + 1) <= K, cand, -jnp.inf)
    return cand, flat, a_of.astype(jnp.float32)


def _pair_counts(v1, v2):
    L = v1.shape[1]
    K = PEER_TOPK
    cand, flat, a_of_f = _pair_candidates(v1, v2)
    aio = lax.broadcasted_iota(jnp.int32, (K, L), 0).astype(jnp.float32)
    cnt = jnp.zeros((K, L), jnp.float32)
    top = v1[0:1] + v2[0:1]
    z = jnp.zeros((1, L), jnp.float32)
    for _ in range(K):
        m = jnp.max(cand, axis=0, keepdims=True)
        f = jnp.min(jnp.where(cand == m, flat, float(K * K)), axis=0, keepdims=True)
        sel = flat == f
        a_sel = jnp.max(jnp.where(sel, a_of_f, 0.0), axis=0, keepdims=True)
        cand = jnp.where(sel, -jnp.inf, cand)
        cnt = cnt + jnp.where(aio == a_sel, 1.0, 0.0)
        z = z + jnp.exp(m - top)
    return cnt, z


def _route_head(s1, s2):
    v1, i1, _ = _top16_rows(s1, False)
    v2, _, rank2 = _top16_rows(s2, True)
    cnt, z = _pair_counts(v1, v2)
    kio = lax.broadcasted_iota(jnp.int32, s1.shape, 0).astype(jnp.float32)
    cnt_i = jnp.zeros_like(s1)
    for a in range(PEER_TOPK):
        cnt_i = jnp.where(kio == i1[a:a + 1], cnt[a:a + 1], cnt_i)
    e1 = jnp.exp(s1 - v1[0:1])
    e2 = jnp.exp(s2 - v2[0:1]) / z
    return cnt_i, e1, rank2, e2


def _mark_top16(s):
    L = s.shape[1]
    aio = lax.broadcasted_iota(jnp.int32, (PEER_TOPK, L), 0)
    vals = jnp.zeros((PEER_TOPK, L), jnp.float32)
    for a in range(PEER_TOPK):
        m = jnp.max(s, axis=0, keepdims=True)
        s = jnp.where(s == m, _MARK + a * _MARK_STEP, s)
        vals = jnp.where(aio == a, m, vals)
    marked = s < _MARK_LIMIT
    rank = jnp.where(marked, (s - _MARK) * (1.0 / _MARK_STEP), float(PEER_TOPK))
    n_marked = jnp.sum(jnp.where(marked, 1.0, 0.0), axis=0, keepdims=True)
    return vals, rank, n_marked


def _pair_counts_distinct(v1, v2):
    L = v1.shape[1]
    K = PEER_TOPK
    orig, _, _ = _pair_candidates(v1, v2)
    cand = orig
    for _ in range(K):
        m = jnp.max(cand, axis=0, keepdims=True)
        cand = jnp.where(cand == m, _MARK, cand)
    sel = cand == _MARK
    self = jnp.where(sel, 1.0, 0.0)
    c8 = self[16:24]
    for b in range(2, 8):
        c8 = c8 + self[8 + 8 * b:16 + 8 * b]
    row8 = lax.broadcasted_iota(jnp.int32, (8, L), 0)
    c8 = c8 + jnp.where(row8 == 0, jnp.sum(self[72:80], axis=0, keepdims=True), 0.0)
    cnt = self[0:16] + jnp.concatenate([c8, jnp.zeros((8, L), jnp.float32)], axis=0)
    top = v1[0:1] + v2[0:1]
    z = jnp.sum(jnp.where(sel, jnp.exp(orig - top), 0.0), axis=0, keepdims=True)
    return cnt, z, jnp.sum(self, axis=0, keepdims=True)


def _route_head_distinct(s1, s2):
    v1, rank1, n1 = _mark_top16(s1)
    v2, rank2, n2 = _mark_top16(s2)
    cnt, z, n3 = _pair_counts_distinct(v1, v2)
    cnt_i = jnp.zeros_like(s1)
    for a in range(PEER_TOPK):
        cnt_i = jnp.where(rank1 == float(a), cnt[a:a + 1], cnt_i)
    e1 = jnp.exp(s1 - v1[0:1])
    e2 = jnp.exp(s2 - v2[0:1]) / z
    k = float(PEER_TOPK)
    floor = jnp.minimum(jnp.min(s1, axis=0, keepdims=True), jnp.min(s2, axis=0, keepdims=True))
    ok = (n1 == k) & (n2 == k) & (n3 == k) & (floor > _SCORE_FLOOR)
    return (cnt_i, e1, rank2, e2), ok


def _router_kernel(x_ref, gb_ref, wqt_ref, k1h_ref, k1l_ref, k2h_ref, k2l_ref,
                   hnt_ref, row_ref, rank2_ref, e2_ref, qh_scr, ql_scr):
    xt = x_ref[...].T
    ms = jnp.mean(xt * xt, axis=0, keepdims=True)
    hnt = (xt * lax.rsqrt(ms + RMS_EPS) * gb_ref[...]).astype(hnt_ref.dtype)
    hnt_ref[...] = hnt
    qt = jnp.dot(wqt_ref[...], hnt, preferred_element_type=jnp.float32)
    q_hi = qt.astype(_MXU)
    qh_scr[...] = q_hi
    ql_scr[...] = (qt - q_hi.astype(jnp.float32)).astype(_MXU)
    L = xt.shape[1]

    heads_per_iter = 2

    def scores(kh_ref, kl_ref, h, r0, ls):
        qh = qh_scr[pl.ds(r0, N_KEYS), ls]
        ql = ql_scr[pl.ds(r0, N_KEYS), ls]
        return (jnp.dot(kh_ref[h], qh, preferred_element_type=jnp.float32)
                + jnp.dot(kh_ref[h], ql, preferred_element_type=jnp.float32)
                + jnp.dot(kl_ref[h], qh, preferred_element_type=jnp.float32))

    def store(res, h, ls):
        cnt_i, e1, rank2, e2 = res
        row_ref[2 * h, :, ls] = cnt_i
        row_ref[2 * h + 1, :, ls] = e1
        rank2_ref[h, :, ls] = rank2.astype(rank2_ref.dtype)
        e2_ref[h, :, ls] = e2.astype(e2_ref.dtype)

    def head_group(hg, carry):
        units = []
        for hh in range(heads_per_iter):
            h = hg * heads_per_iter + hh
            r1 = pl.multiple_of(h * 2 * N_KEYS, 2 * N_KEYS)
            for lt in range(L // LANES):
                ls = slice(lt * LANES, (lt + 1) * LANES)
                s1 = scores(k1h_ref, k1l_ref, h, r1, ls)
                s2 = scores(k2h_ref, k2l_ref, h, r1 + N_KEYS, ls)
                res, ok = _route_head_distinct(s1, s2)
                store(res, h, ls)
                units.append((h, ls, s1, s2, jnp.sum(jnp.where(ok, 0.0, 1.0))))
        for h, ls, s1, s2, n_bad in units:
            @pl.when(n_bad > 0.0)
            def _():
                store(_route_head(s1, s2), h, ls)
        return carry

    lax.fori_loop(0, PEER_HEADS // heads_per_iter, head_group, 0)


def _peer_router(x2, g, wqt, k1h, k1l, k2h, k2l, tm=256):
    T, D = x2.shape
    NQ = wqt.shape[0]
    H = PEER_HEADS
    gb = jnp.broadcast_to(g.reshape(D, 1), (D, tm))
    kspec = pl.BlockSpec((H, N_KEYS, N_KEYS), lambda i: (0, 0, 0))
    return pl.pallas_call(
        _router_kernel,
        grid=(T // tm,),
        in_specs=[pl.BlockSpec((tm, D), lambda i: (i, 0)),
                  pl.BlockSpec((D, tm), lambda i: (0, 0)),
                  pl.BlockSpec((NQ, D), lambda i: (0, 0)),
                  kspec, kspec, kspec, kspec],
        out_specs=[pl.BlockSpec((D, tm), lambda i: (0, i)),
                   pl.BlockSpec((2 * H, N_KEYS, tm), lambda i: (0, 0, i)),
                   pl.BlockSpec((H, N_KEYS, tm), lambda i: (0, 0, i)),
                   pl.BlockSpec((H, N_KEYS, tm), lambda i: (0, 0, i))],
        out_shape=[jax.ShapeDtypeStruct((D, T), _MXU),
                   jax.ShapeDtypeStruct((2 * H, N_KEYS, T), jnp.float32),
                   jax.ShapeDtypeStruct((H, N_KEYS, T), _GATE),
                   jax.ShapeDtypeStruct((H, N_KEYS, T), _GATE)],
        scratch_shapes=[pltpu.VMEM((NQ, tm), _MXU), pltpu.VMEM((NQ, tm), _MXU)],
        compiler_params=_params(("parallel",)),
        name="peer_router",
    )(x2, gb, wqt, k1h, k1l, k2h, k2l)


def _gelu(a):
    return 0.5 * a * (1.0 + lax.erf(a * (2.0 ** -0.5)))


def _peer_dense_kernel(*refs, n_e, final_norm):
    refs = list(refs)
    hnt_ref, u_ref, vt_ref, row_ref, rank2_ref, e2_ref, x_ref = refs[:7]
    gf_ref = refs[7] if final_norm else None
    out_ref, acc_ref, at_ref, gt_ref = refs[-4:]
    s = pl.program_id(0)
    n_pairs = pl.num_programs(0) - 2
    eb = u_ref.shape[0]
    sub = eb // N_KEYS

    @pl.when(s == 0)
    def _():
        at_ref[...] = jnp.zeros_like(at_ref)
        gt_ref[...] = jnp.zeros_like(gt_ref)

    slot = s % 2
    e_g = jnp.clip(s - 1, 0, n_pairs - 1) % n_e
    e_y = jnp.clip(s - 2, 0, n_pairs - 1) % n_e

    at_ref[slot] = jnp.dot(u_ref[...], hnt_ref[...], preferred_element_type=jnp.float32)

    for ii in range(sub):
        i = e_g * sub + ii
        rs = slice(ii * N_KEYS, (ii + 1) * N_KEYS)
        act = _gelu(at_ref[1 - slot, rs, :])
        gate = None
        for h in range(PEER_HEADS):
            cnt = row_ref[2 * h, pl.ds(i, 1), :].astype(_GATE)
            e1 = row_ref[2 * h + 1, pl.ds(i, 1), :].astype(_GATE)
            term = jnp.where(rank2_ref[h] < cnt, e2_ref[h], jnp.zeros((), _GATE)) * e1
            gate = term if gate is None else gate + term
        gt_ref[1 - slot, rs, :] = (gate * act.astype(_GATE)).astype(gt_ref.dtype)

    y = jnp.dot(vt_ref[...], gt_ref[slot], preferred_element_type=jnp.float32)
    acc_ref[...] = jnp.where(e_y == 0, y, acc_ref[...] + y)

    @pl.when((e_y == n_e - 1) & (s >= 2))
    def _():
        res = x_ref[...] + acc_ref[...].T
        if final_norm:
            res = _rmsnorm(res, gf_ref[...])
        out_ref[...] = res


def _peer_dense(hnt, u, vt, rowdat, rank2, e2, x2, g_final=None, tm=512, eb=1024):
    T, D = x2.shape
    H = PEER_HEADS
    n_e = N_EXPERTS // eb
    n_pairs = (T // tm) * n_e
    final_norm = g_final is not None

    def pair(s, lag):
        p = jnp.clip(s - lag, 0, n_pairs - 1)
        return p // n_e, p % n_e

    in_specs = [pl.BlockSpec((D, tm), lambda s: (0, pair(s, 0)[0])),
                pl.BlockSpec((eb, D), lambda s: (pair(s, 0)[1], 0)),
                pl.BlockSpec((D, eb), lambda s: (0, pair(s, 2)[1])),
                pl.BlockSpec((2 * H, N_KEYS, tm), lambda s: (0, 0, pair(s, 1)[0])),
                pl.BlockSpec((H, N_KEYS, tm), lambda s: (0, 0, pair(s, 1)[0])),
                pl.BlockSpec((H, N_KEYS, tm), lambda s: (0, 0, pair(s, 1)[0])),
                pl.BlockSpec((tm, D), lambda s: (pair(s, 2)[0], 0))]
    args = [hnt, u, vt, rowdat, rank2, e2, x2]
    if final_norm:
        in_specs.append(pl.BlockSpec((1, D), lambda s: (0, 0)))
        args.append(g_final)
    return pl.pallas_call(
        functools.partial(_peer_dense_kernel, n_e=n_e, final_norm=final_norm),
        grid=(n_pairs + 2,),
        in_specs=in_specs,
        out_specs=pl.BlockSpec((tm, D), lambda s: (pair(s, 2)[0], 0)),
        out_shape=jax.ShapeDtypeStruct((T, D), jnp.float32),
        scratch_shapes=[pltpu.VMEM((D, tm), jnp.float32),
                        pltpu.VMEM((2, eb, tm), jnp.float32),
                        pltpu.VMEM((2, eb, tm), _MXU)],
        compiler_params=_params(("arbitrary",)),
        name="peer_dense_final" if final_norm else "peer_dense",
    )(*args)


def _split_hi_lo(w):
    hi = w.astype(_MXU)
    lo = (w - hi.astype(jnp.float32)).astype(_MXU)
    return hi, lo


def _peer(x2, g, wqt, sk1, sk2, u, vt, g_final=None):
    k1h, k1l = _split_hi_lo(sk1)
    k2h, k2l = _split_hi_lo(sk2)
    hnt, rowdat, rank2, e2 = _peer_router(x2, g, wqt, k1h, k1l, k2h, k2l)
    return _peer_dense(hnt, u, vt, rowdat, rank2, e2, x2, g_final)


def _trunk(x, p):
    B, S, D = x.shape
    x2 = x.reshape(B * S, D)
    qkv = _norm_proj(x2, p["ln_mix"][0:1], p["w_qkv_na"], S)
    o = _na_attention(qkv.reshape(B, S, 3 * D), p["na_bias"], B, S)
    x2 = _proj_residual(o.reshape(B * S, D), p["w_o_na"], x2)
    x2 = _peer(x2, p["ln_ffn"][0:1], p["wqt"][0], p["sk1"][0], p["sk2"][0], p["u"][0], p["vt"][0])
    tables = _rope_tables(S)
    outs, lses = [], []
    for gi, (window, dil) in enumerate(DIL_CONFIGS):
        qkv_g = _norm_proj(x2, p["ln_mix"][1:2], p["w_qkv_dil"], S, col0=3 * gi, dil=dil, rope_tables=tables)
        o_g, lse_g = _dil_group(qkv_g, B, S, window, dil)
        outs.append(o_g)
        lses.append(lse_g)
    x2 = _dil_merge_proj(outs, lses, x2, p["w_o_dil"])
    x2 = _peer(x2, p["ln_ffn"][1:2], p["wqt"][1], p["sk1"][1], p["sk2"][1], p["u"][1], p["vt"][1],
               g_final=p["ln_final"])
    return x2.reshape(B, S, D)


def kernel(x_prompt, x_sample, ln_mix, ln_ffn, ln_final, w_qkv_na, rpb_na, w_o_na, w_qkv_dil, w_o_dil,
           w_query_peer, subkeys1_peer, subkeys2_peer, u_peer, v_peer):
    p = {
        "ln_mix": ln_mix, "ln_ffn": ln_ffn, "ln_final": ln_final.reshape(1, -1),
        "w_qkv_na": w_qkv_na[0].astype(_MXU), "w_o_na": w_o_na[0].astype(_MXU),
        "na_bias": _na_bias_table(rpb_na[0]),
        "w_qkv_dil": w_qkv_dil[0].astype(_MXU), "w_o_dil": w_o_dil[0].astype(_MXU),
        "wqt": jnp.swapaxes(w_query_peer, 1, 2).astype(_MXU),
        "sk1": subkeys1_peer, "sk2": subkeys2_peer,
        "u": u_peer.astype(_MXU), "vt": jnp.swapaxes(v_peer, 1, 2).astype(_MXU),
    }
    return _trunk(x_prompt, p), _trunk(x_sample, p)
```

```python
import functools

import numpy as np
import jax
import jax.numpy as jnp
from jax import lax
from jax.experimental import pallas as pl
from jax.experimental.pallas import tpu as pltpu

D_MODEL = 1024
HEAD_DIM = 64
N_HEADS = 16
GRID_W = 64
NA_KH = 8
NA_KW = 16
NA_ROWS_PER_STEP = 4
NA_HEAD_LANES = 512
DIL_CONFIGS = ((128, 1), (512, 4), (2048, 16))
DIL_BLOCK = 128
ROPE_THETA = 10000.0
PEER_HEADS = 8
N_KEYS = 128
N_EXPERTS = N_KEYS * N_KEYS
PEER_TOPK = 16
RMS_EPS = 1e-6
NEG_INF = -1e30
LANES = 128

_MXU = jnp.bfloat16
_GATE = jnp.bfloat16
_VMEM_LIMIT = 48 * 1024 * 1024

_NT = (((1,), (1,)), ((), ()))

_MARK = -(2.0 ** 127)
_MARK_STEP = 2.0 ** 104
_MARK_LIMIT = -(2.0 ** 126)
_SCORE_FLOOR = -(2.0 ** 120)


def _params(sem):
    return pltpu.CompilerParams(dimension_semantics=sem, vmem_limit_bytes=_VMEM_LIMIT)


def _rmsnorm(x, g):
    ms = jnp.mean(x * x, axis=-1, keepdims=True)
    return x * lax.rsqrt(ms + RMS_EPS) * g


def _norm_proj_kernel(*refs, rope, dil):
    refs = list(refs)
    x_ref, g_ref, w_ref = refs[:3]
    cos_ref, sin_ref = (refs[3], refs[4]) if rope else (None, None)
    o_ref, xn_ref = refs[5:7] if rope else refs[3:5]
    y_ref = refs[-1] if (rope or dil > 1) else None
    j = pl.program_id(1)

    @pl.when(j == 0)
    def _():
        xn_ref[...] = _rmsnorm(x_ref[...], g_ref[...]).astype(xn_ref.dtype)

    acc = jnp.dot(xn_ref[...], w_ref[...], preferred_element_type=jnp.float32)
    scale = jnp.where(j == 0, HEAD_DIM ** -0.5, 1.0).astype(jnp.float32)

    if y_ref is None:
        o_ref[...] = (acc * scale).astype(o_ref.dtype)
        return

    n_lt = acc.shape[1] // LANES
    if rope:
        @pl.when(j == 2)
        def _():
            for t in range(n_lt):
                y_ref[t] = acc[:, t * LANES:(t + 1) * LANES]

        @pl.when(j != 2)
        def _():
            cos = cos_ref[...]
            sin = sin_ref[...]
            lane = lax.broadcasted_iota(jnp.int32, cos.shape, 1)
            first_half = (lane % HEAD_DIM) < HEAD_DIM // 2
            for t in range(n_lt):
                xt = acc[:, t * LANES:(t + 1) * LANES]
                partner = jnp.where(first_half,
                                    pltpu.roll(xt, LANES - HEAD_DIM // 2, 1),
                                    pltpu.roll(xt, HEAD_DIM // 2, 1))
                y_ref[t] = (xt * cos + partner * sin) * scale
    else:
        for t in range(n_lt):
            y_ref[t] = acc[:, t * LANES:(t + 1) * LANES] * scale

    rows = acc.shape[0] // dil
    for r in range(dil):
        for t in range(n_lt):
            src = y_ref[t] if dil == 1 else y_ref[t, pl.ds(r, rows, stride=dil), :]
            c0 = r * D_MODEL + t * LANES
            o_ref[:, c0:c0 + LANES] = src.astype(o_ref.dtype)


def _norm_proj(x2, g, w, seq_len, col0=0, dil=1, rope_tables=None, tm=512):
    T, D = x2.shape
    rope = rope_tables is not None
    in_specs = [
        pl.BlockSpec((tm, D), lambda i, j: (i, 0)),
        pl.BlockSpec((1, D), lambda i, j: (0, 0)),
        pl.BlockSpec((D, D), lambda i, j: (0, col0 + j)),
    ]
    args = [x2, g, w]
    if rope:
        nper = seq_len // tm
        tab = pl.BlockSpec((tm, LANES), lambda i, j: (i % nper, 0))
        in_specs += [tab, tab]
        args += list(rope_tables)
    scratch = [pltpu.VMEM((tm, D), _MXU)]
    if rope or dil > 1:
        scratch.append(pltpu.VMEM((D // LANES, tm, LANES), jnp.float32))
    return pl.pallas_call(
        functools.partial(_norm_proj_kernel, rope=rope, dil=dil),
        grid=(T // tm, 3),
        in_specs=in_specs,
        out_specs=pl.BlockSpec((tm // dil, dil * D), lambda i, j: (i, j)),
        out_shape=jax.ShapeDtypeStruct((T // dil, 3 * dil * D), _MXU),
        scratch_shapes=scratch,
        compiler_params=_params(("parallel", "arbitrary")),
        name=f"norm_proj_rope_d{dil}" if rope else "norm_proj",
    )(*args)


def _rope_tables(seq_len):
    half = HEAD_DIM // 2
    freqs = ROPE_THETA ** (-jnp.arange(half, dtype=jnp.float32) / half)
    ang = jnp.arange(seq_len, dtype=jnp.float32)[:, None] * freqs[None, :]
    cos = jnp.cos(ang)
    sin = jnp.sin(ang)
    cos_t = jnp.tile(cos, (1, LANES // half))
    sin_t = jnp.tile(jnp.concatenate([-sin, sin], axis=1), (1, LANES // HEAD_DIM))
    return cos_t, sin_t


def _na_kernel(q_ref, k0_ref, k1_ref, k2_ref, v0_ref, v1_ref, v2_ref, bias_ref, o_ref,
               kbuf, vbuf, s_scr, m_scr, e_scr, *, rows):
    g = pl.program_id(2)
    rp = NA_ROWS_PER_STEP
    blk = rp * GRID_W
    nblk = rows // rp
    b0 = jnp.clip(g - 1, 0, nblk - 3)
    for i, (kr, vr) in enumerate(((k0_ref, v0_ref), (k1_ref, v1_ref), (k2_ref, v2_ref))):
        kbuf[i * blk:(i + 1) * blk, :] = kr[0]
        vbuf[i * blk:(i + 1) * blk, :] = vr[0]
    nkeys = NA_KH * GRID_W
    ones = jnp.ones((nkeys, LANES), kbuf.dtype)
    nh = q_ref.shape[2] // HEAD_DIM
    offs = []
    for r in range(rp):
        r_abs = g * rp + r
        rs = jnp.clip(r_abs - NA_KH // 2, 0, rows - NA_KH)
        off = pl.multiple_of((rs - b0 * rp) * GRID_W, GRID_W)
        offs.append(off)
        var = rs - r_abs + NA_KH - 1
        for h in range(nh):
            hs = slice(h * HEAD_DIM, (h + 1) * HEAD_DIM)
            qh = q_ref[0, r * GRID_W:(r + 1) * GRID_W, hs]
            s = lax.dot_general(qh, kbuf[pl.ds(off, nkeys), hs], _NT, preferred_element_type=jnp.float32)
            s = s + bias_ref[var, h]
            s_scr[r * nh + h] = s
            m_scr[r * nh + h] = jnp.broadcast_to(jnp.max(s, axis=-1, keepdims=True), (GRID_W, LANES))
    for u in range(rp * nh):
        mb = m_scr[u]
        e_scr[u] = jnp.exp(s_scr[u] - jnp.concatenate([mb] * (nkeys // LANES), axis=1)).astype(e_scr.dtype)
    for r in range(rp):
        for h in range(nh):
            hs = slice(h * HEAD_DIM, (h + 1) * HEAD_DIM)
            e = e_scr[r * nh + h]
            l = jnp.dot(e, ones, preferred_element_type=jnp.float32)
            o = jnp.dot(e, vbuf[pl.ds(offs[r], nkeys), hs], preferred_element_type=jnp.float32)
            o_ref[0, r * GRID_W:(r + 1) * GRID_W, hs] = (o / l[:, :HEAD_DIM]).astype(o_ref.dtype)


def _na_bias_table(rpb):
    qc = np.arange(GRID_W)[:, None]
    kc = np.arange(GRID_W)[None, :]
    cs = np.clip(qc - NA_KW // 2, 0, GRID_W - NA_KW)
    ok = (kc >= cs) & (kc < cs + NA_KW)
    col_neg = jnp.asarray(np.where(ok, 0.0, NEG_INF), jnp.float32)
    col_off = np.clip(kc - qc + NA_KW - 1, 0, 2 * NA_KW - 2)
    tab = rpb.astype(jnp.float32)[:, :, col_off] + col_neg[None, None]
    variants = []
    for var in range(NA_KH):
        t = tab[:, var:var + NA_KH]
        variants.append(jnp.transpose(t, (0, 2, 1, 3)).reshape(N_HEADS, GRID_W, NA_KH * GRID_W))
    return jnp.stack(variants, axis=0)


def _na_attention(qkv, bias_tab, B, S):
    rows = S // GRID_W
    rp = NA_ROWS_PER_STEP
    blk = rp * GRID_W
    nblk = rows // rp
    wl = NA_HEAD_LANES
    nhp = D_MODEL // wl
    hp_heads = wl // HEAD_DIM

    def kv_spec(which, i):
        return pl.BlockSpec(
            (1, blk, wl),
            lambda hp, b, g: (b, jnp.clip(g - 1, 0, nblk - 3) + i, which * nhp + hp))

    return pl.pallas_call(
        functools.partial(_na_kernel, rows=rows),
        grid=(nhp, B, nblk),
        in_specs=[pl.BlockSpec((1, blk, wl), lambda hp, b, g: (b, g, hp))]
        + [kv_spec(1, i) for i in range(3)] + [kv_spec(2, i) for i in range(3)]
        + [pl.BlockSpec((NA_KH, hp_heads, GRID_W, NA_KH * GRID_W), lambda hp, b, g: (0, hp, 0, 0))],
        out_specs=pl.BlockSpec((1, blk, wl), lambda hp, b, g: (b, g, hp)),
        out_shape=jax.ShapeDtypeStruct((B, S, D_MODEL), qkv.dtype),
        scratch_shapes=[pltpu.VMEM((3 * blk, wl), qkv.dtype),
                        pltpu.VMEM((3 * blk, wl), qkv.dtype),
                        pltpu.VMEM((rp * hp_heads, GRID_W, NA_KH * GRID_W), jnp.float32),
                        pltpu.VMEM((rp * hp_heads, GRID_W, LANES), jnp.float32),
                        pltpu.VMEM((rp * hp_heads, GRID_W, NA_KH * GRID_W), qkv.dtype)],
        compiler_params=_params(("arbitrary", "arbitrary", "arbitrary")),
        name="na_attention",
    )(qkv, qkv, qkv, qkv, qkv, qkv, qkv, bias_tab)


def _proj_res_kernel(o_ref, w_ref, x_ref, out_ref):
    out_ref[...] = x_ref[...] + jnp.dot(o_ref[...], w_ref[...], preferred_element_type=jnp.float32)


def _proj_residual(o2, w, x2, tm=512):
    T, D = x2.shape
    return pl.pallas_call(
        _proj_res_kernel,
        grid=(T // tm,),
        in_specs=[pl.BlockSpec((tm, D), lambda i: (i, 0)),
                  pl.BlockSpec((D, D), lambda i: (0, 0)),
                  pl.BlockSpec((tm, D), lambda i: (i, 0))],
        out_specs=pl.BlockSpec((tm, D), lambda i: (i, 0)),
        out_shape=jax.ShapeDtypeStruct((T, D), jnp.float32),
        compiler_params=_params(("parallel",)),
        name="proj_residual",
    )(o2, w, x2)


def _dil_kernel(q_ref, kp_ref, kc_ref, kn_ref, vp_ref, vc_ref, vn_ref, o_ref, lse_ref, kbuf, vbuf,
                s_scr, m_scr, e_scr, *,
                n_blocks, half):
    jb = pl.program_id(2)
    C = q_ref.shape[1]
    hc = C // 2
    nk = 2 * C
    for buf, (p_ref, c_ref, n_ref) in ((kbuf, (kp_ref, kc_ref, kn_ref)), (vbuf, (vp_ref, vc_ref, vn_ref))):
        buf[0:hc, :] = p_ref[0]
        buf[hc:hc + C, :] = c_ref[0]
        buf[hc + C:nk, :] = n_ref[0]
    qi = lax.broadcasted_iota(jnp.int32, (C, nk), 0)
    ci = lax.broadcasted_iota(jnp.int32, (C, nk), 1)
    lo = jnp.where(jb == 0, hc, 0)
    hi = jnp.where(jb == n_blocks - 1, hc + C, nk)
    mask = (jnp.abs(ci - hc - qi) <= half) & (ci >= lo) & (ci < hi)
    ones = jnp.ones((nk, LANES), kbuf.dtype)
    lane = lax.broadcasted_iota(jnp.int32, (C, LANES), 1)
    lse_all = jnp.zeros((C, LANES), jnp.float32)
    for h in range(N_HEADS):
        hs = slice(h * HEAD_DIM, (h + 1) * HEAD_DIM)
        s = lax.dot_general(q_ref[0, :, hs], kbuf[:, hs], _NT, preferred_element_type=jnp.float32)
        s = jnp.where(mask, s, NEG_INF)
        s_scr[h] = s
        m = jnp.max(jnp.maximum(s[:, :C], s[:, C:]), axis=-1, keepdims=True)
        m_scr[h] = jnp.broadcast_to(m, (C, LANES))
    for h in range(N_HEADS):
        mb = m_scr[h]
        e_scr[h] = jnp.exp(s_scr[h] - jnp.concatenate([mb, mb], axis=1)).astype(e_scr.dtype)
    for h in range(N_HEADS):
        hs = slice(h * HEAD_DIM, (h + 1) * HEAD_DIM)
        e = e_scr[h]
        l = jnp.dot(e, ones, preferred_element_type=jnp.float32)
        o = jnp.dot(e, vbuf[:, hs], preferred_element_type=jnp.float32)
        o_ref[0, :, hs] = (o / l[:, :HEAD_DIM]).astype(o_ref.dtype)
        lse_all = jnp.where(lane == h, m_scr[h] + jnp.log(l), lse_all)
    lse_ref[0] = lse_all


def _dil_group(qkv_g, B, S, window, dil):
    n = S // dil
    C = DIL_BLOCK
    hc = C // 2
    nb = n // C
    half = window // (2 * dil)
    assert half <= hc and n % C == 0
    qkv_v = qkv_g.reshape(B, n, qkv_g.shape[-1])

    def cur(which):
        return pl.BlockSpec((1, C, D_MODEL), lambda b, r, jb: (b, jb, which * dil + r))

    def edge(which, shift):
        def idx(b, r, jb):
            return (b, jnp.clip(2 * jb + shift, 0, 2 * nb - 1), which * dil + r)
        return pl.BlockSpec((1, hc, D_MODEL), idx)

    o, lse = pl.pallas_call(
        functools.partial(_dil_kernel, n_blocks=nb, half=half),
        grid=(B, dil, nb),
        in_specs=[cur(0), edge(1, -1), cur(1), edge(1, 2), edge(2, -1), cur(2), edge(2, 2)],
        out_specs=[pl.BlockSpec((1, C, D_MODEL), lambda b, r, jb: (b, jb, r)),
                   pl.BlockSpec((1, C, LANES), lambda b, r, jb: (b, jb, r))],
        out_shape=[jax.ShapeDtypeStruct((B, n, dil * D_MODEL), jnp.float32),
                   jax.ShapeDtypeStruct((B, n, dil * LANES), jnp.float32)],
        scratch_shapes=[pltpu.VMEM((2 * C, D_MODEL), qkv_g.dtype), pltpu.VMEM((2 * C, D_MODEL), qkv_g.dtype),
                        pltpu.VMEM((N_HEADS, C, 2 * C), jnp.float32),
                        pltpu.VMEM((N_HEADS, C, LANES), jnp.float32),
                        pltpu.VMEM((N_HEADS, C, 2 * C), qkv_g.dtype)],
        compiler_params=_params(("parallel", "parallel", "arbitrary")),
        name=f"dil_group_d{dil}",
    )(qkv_v, qkv_v, qkv_v, qkv_v, qkv_v, qkv_v, qkv_v)
    return o.reshape(B * n, dil * D_MODEL), lse.reshape(B * n, dil * LANES)


def _dil_merge_kernel(o1_ref, o2_ref, o3_ref, l1_ref, l2_ref, l3_ref, x_ref, w_ref, exp_ref, out_ref,
                      o_scr, l_scr):
    tm = x_ref.shape[0]
    outs, lses = [], []
    for gi, (o_ref, l_ref) in enumerate(((o1_ref, l1_ref), (o2_ref, l2_ref), (o3_ref, l3_ref))):
        dil = DIL_CONFIGS[gi][1]
        if dil == 1:
            outs.append(o_ref[...])
            lses.append(l_ref[...])
            continue
        rows = tm // dil
        n_lt = D_MODEL // LANES
        for r in range(dil):
            for t in range(n_lt):
                c0 = r * D_MODEL + t * LANES
                o_scr[gi, t, pl.ds(r, rows, stride=dil), :] = o_ref[:, c0:c0 + LANES]
            l_scr[gi, pl.ds(r, rows, stride=dil), :] = l_ref[:, r * LANES:(r + 1) * LANES]
        outs.append(jnp.concatenate([o_scr[gi, t] for t in range(n_lt)], axis=1))
        lses.append(l_scr[gi])
    m = jnp.maximum(jnp.maximum(lses[0], lses[1]), lses[2])
    es = [jnp.exp(l - m) for l in lses]
    z = es[0] + es[1] + es[2]
    merged = None
    for e, o in zip(es, outs):
        w = e / z
        hi = w.astype(_MXU)
        lo = (w - hi.astype(jnp.float32)).astype(_MXU)
        wide = (jnp.dot(hi, exp_ref[...], preferred_element_type=jnp.float32)
                + jnp.dot(lo, exp_ref[...], preferred_element_type=jnp.float32))
        term = wide * o
        merged = term if merged is None else merged + term
    out_ref[...] = x_ref[...] + jnp.dot(merged.astype(_MXU), w_ref[...],
                                        preferred_element_type=jnp.float32)


def _dil_merge_proj(outs, lses, x2, w, tm=256):
    T, D = x2.shape
    expand = np.zeros((LANES, D), np.float32)
    for h in range(N_HEADS):
        expand[h, h * HEAD_DIM:(h + 1) * HEAD_DIM] = 1.0
    expand = jnp.asarray(expand, _MXU)
    row = pl.BlockSpec((tm, D), lambda i: (i, 0))
    o_specs = [pl.BlockSpec((tm // d, d * D), lambda i: (i, 0)) for _, d in DIL_CONFIGS]
    l_specs = [pl.BlockSpec((tm // d, d * LANES), lambda i: (i, 0)) for _, d in DIL_CONFIGS]
    ng = len(DIL_CONFIGS)
    return pl.pallas_call(
        _dil_merge_kernel,
        grid=(T // tm,),
        in_specs=o_specs + l_specs + [row, pl.BlockSpec((D, D), lambda i: (0, 0)),
                                      pl.BlockSpec((LANES, D), lambda i: (0, 0))],
        out_specs=row,
        out_shape=jax.ShapeDtypeStruct((T, D), jnp.float32),
        scratch_shapes=[pltpu.VMEM((ng, D // LANES, tm, LANES), jnp.float32),
                        pltpu.VMEM((ng, tm, LANES), jnp.float32)],
        compiler_params=_params(("parallel",)),
        name="dil_merge_proj",
    )(*outs, *lses, x2, w, expand)


def _top16_rows(s, want_rank):
    K, L = s.shape
    kio = lax.broadcasted_iota(jnp.int32, (K, L), 0).astype(jnp.float32)
    aio = lax.broadcasted_iota(jnp.int32, (PEER_TOPK, L), 0)
    vals = jnp.zeros((PEER_TOPK, L), jnp.float32)
    idxs = jnp.zeros((PEER_TOPK, L), jnp.float32)
    rank = jnp.full((K, L), float(PEER_TOPK), jnp.float32) if want_rank else None
    for a in range(PEER_TOPK):
        m = jnp.max(s, axis=0, keepdims=True)
        idx = jnp.min(jnp.where(s == m, kio, float(K)), axis=0, keepdims=True)
        sel = kio == idx
        if want_rank:
            rank = jnp.where(sel, float(a), rank)
        s = jnp.where(sel, -jnp.inf, s)
        vals = jnp.where(aio == a, m, vals)
        idxs = jnp.where(aio == a, idx, idxs)
    return vals, idxs, rank


def _pair_candidates(v1, v2):
    L = v1.shape[1]
    K = PEER_TOPK
    blocks = [v1 + v2[0:1]]
    for b in range(1, 8):
        blocks.append(v1[0:8] + v2[b:b + 1])
    blocks.append(v1[0:1] + v2[8:16])
    cand = jnp.concatenate(blocks, axis=0)
    R = cand.shape[0]
    r = lax.broadcasted_iota(jnp.int32, (R, L), 0)
    a_mid = (r - 16) & 7
    b_mid = ((r - 16) >> 3) + 1
    a_of = jnp.where(r < 16, r, jnp.where(r < 72, a_mid, 0))
    b_of = jnp.where(r < 16, 0, jnp.where(r < 72, b_mid, r - 64))
    flat = (a_of * K + b_of).astype(jnp.float32)
    cand = jnp.where((a_of + 1) * (b_of + 1) <= K, cand, -jnp.inf)
    return cand, flat, a_of.astype(jnp.float32)


def _pair_counts(v1, v2):
    L = v1.shape[1]
    K = PEER_TOPK
    cand, flat, a_of_f = _pair_candidates(v1, v2)
    aio = lax.broadcasted_iota(jnp.int32, (K, L), 0).astype(jnp.float32)
    cnt = jnp.zeros((K, L), jnp.float32)
    top = v1[0:1] + v2[0:1]
    z = jnp.zeros((1, L), jnp.float32)
    for _ in range(K):
        m = jnp.max(cand, axis=0, keepdims=True)
        f = jnp.min(jnp.where(cand == m, flat, float(K * K)), axis=0, keepdims=True)
        sel = flat == f
        a_sel = jnp.max(jnp.where(sel, a_of_f, 0.0), axis=0, keepdims=True)
        cand = jnp.where(sel, -jnp.inf, cand)
        cnt = cnt + jnp.where(aio == a_sel, 1.0, 0.0)
        z = z + jnp.exp(m - top)
    return cnt, z


def _route_head(s1, s2):
    v1, i1, _ = _top16_rows(s1, False)
    v2, _, rank2 = _top16_rows(s2, True)
    cnt, z = _pair_counts(v1, v2)
    kio = lax.broadcasted_iota(jnp.int32, s1.shape, 0).astype(jnp.float32)
    cnt_i = jnp.zeros_like(s1)
    for a in range(PEER_TOPK):
        cnt_i = jnp.where(kio == i1[a:a + 1], cnt[a:a + 1], cnt_i)
    e1 = jnp.exp(s1 - v1[0:1])
    e2 = jnp.exp(s2 - v2[0:1]) / z
    return cnt_i, e1, rank2, e2


def _mark_top16(s):
    L = s.shape[1]
    aio = lax.broadcasted_iota(jnp.int32, (PEER_TOPK, L), 0)
    vals = jnp.zeros((PEER_TOPK, L), jnp.float32)
    for a in range(PEER_TOPK):
        m = jnp.max(s, axis=0, keepdims=True)
        s = jnp.where(s == m, _MARK + a * _MARK_STEP, s)
        vals = jnp.where(aio == a, m, vals)
    marked = s < _MARK_LIMIT
    rank = jnp.where(marked, (s - _MARK) * (1.0 / _MARK_STEP), float(PEER_TOPK))
    n_marked = jnp.sum(jnp.where(marked, 1.0, 0.0), axis=0, keepdims=True)
    return vals, rank, n_marked


def _pair_counts_distinct(v1, v2):
    L = v1.shape[1]
    K = PEER_TOPK
    orig, _, _ = _pair_candidates(v1, v2)
    cand = orig
    for _ in range(K):
        m = jnp.max(cand, axis=0, keepdims=True)
        cand = jnp.where(cand == m, _MARK, cand)
    sel = cand == _MARK
    self = jnp.where(sel, 1.0, 0.0)
    c8 = self[16:24]
    for b in range(2, 8):
        c8 = c8 + self[8 + 8 * b:16 + 8 * b]
    row8 = lax.broadcasted_iota(jnp.int32, (8, L), 0)
    c8 = c8 + jnp.where(row8 == 0, jnp.sum(self[72:80], axis=0, keepdims=True), 0.0)
    cnt = self[0:16] + jnp.concatenate([c8, jnp.zeros((8, L), jnp.float32)], axis=0)
    top = v1[0:1] + v2[0:1]
    z = jnp.sum(jnp.where(sel, jnp.exp(orig - top), 0.0), axis=0, keepdims=True)
    return cnt, z, jnp.sum(self, axis=0, keepdims=True)


def _route_head_distinct(s1, s2):
    v1, rank1, n1 = _mark_top16(s1)
    v2, rank2, n2 = _mark_top16(s2)
    cnt, z, n3 = _pair_counts_distinct(v1, v2)
    cnt_i = jnp.zeros_like(s1)
    for a in range(PEER_TOPK):
        cnt_i = jnp.where(rank1 == float(a), cnt[a:a + 1], cnt_i)
    e1 = jnp.exp(s1 - v1[0:1])
    e2 = jnp.exp(s2 - v2[0:1]) / z
    k = float(PEER_TOPK)
    floor = jnp.minimum(jnp.min(s1, axis=0, keepdims=True), jnp.min(s2, axis=0, keepdims=True))
    ok = (n1 == k) & (n2 == k) & (n3 == k) & (floor > _SCORE_FLOOR)
    return (cnt_i, e1, rank2, e2), ok


def _router_kernel(x_ref, gb_ref, wqt_ref, k1h_ref, k1l_ref, k2h_ref, k2l_ref,
                   hnt_ref, row_ref, rank2_ref, e2_ref, qh_scr, ql_scr):
    xt = x_ref[...].T
    ms = jnp.mean(xt * xt, axis=0, keepdims=True)
    hnt = (xt * lax.rsqrt(ms + RMS_EPS) * gb_ref[...]).astype(hnt_ref.dtype)
    hnt_ref[...] = hnt
    qt = jnp.dot(wqt_ref[...], hnt, preferred_element_type=jnp.float32)
    q_hi = qt.astype(_MXU)
    qh_scr[...] = q_hi
    ql_scr[...] = (qt - q_hi.astype(jnp.float32)).astype(_MXU)
    L = xt.shape[1]

    heads_per_iter = 4

    def scores(kh_ref, kl_ref, h, r0, ls):
        qh = qh_scr[pl.ds(r0, N_KEYS), ls]
        ql = ql_scr[pl.ds(r0, N_KEYS), ls]
        return (jnp.dot(kh_ref[h], qh, preferred_element_type=jnp.float32)
                + jnp.dot(kh_ref[h], ql, preferred_element_type=jnp.float32)
                + jnp.dot(kl_ref[h], qh, preferred_element_type=jnp.float32))

    def store(res, h, ls):
        cnt_i, e1, rank2, e2 = res
        row_ref[2 * h, :, ls] = cnt_i
        row_ref[2 * h + 1, :, ls] = e1
        rank2_ref[h, :, ls] = rank2.astype(rank2_ref.dtype)
        e2_ref[h, :, ls] = e2.astype(e2_ref.dtype)

    def head_group(hg, carry):
        units = []
        for hh in range(heads_per_iter):
            h = hg * heads_per_iter + hh
            r1 = pl.multiple_of(h * 2 * N_KEYS, 2 * N_KEYS)
            for lt in range(L // LANES):
                ls = slice(lt * LANES, (lt + 1) * LANES)
                s1 = scores(k1h_ref, k1l_ref, h, r1, ls)
                s2 = scores(k2h_ref, k2l_ref, h, r1 + N_KEYS, ls)
                res, ok = _route_head_distinct(s1, s2)
                store(res, h, ls)
                units.append((h, ls, s1, s2, jnp.sum(jnp.where(ok, 0.0, 1.0))))
        for h, ls, s1, s2, n_bad in units:
            @pl.when(n_bad > 0.0)
            def _():
                store(_route_head(s1, s2), h, ls)
        return carry

    lax.fori_loop(0, PEER_HEADS // heads_per_iter, head_group, 0)


def _peer_router(x2, g, wqt, k1h, k1l, k2h, k2l, tm=256):
    T, D = x2.shape
    NQ = wqt.shape[0]
    H = PEER_HEADS
    gb = jnp.broadcast_to(g.reshape(D, 1), (D, tm))
    kspec = pl.BlockSpec((H, N_KEYS, N_KEYS), lambda i: (0, 0, 0))
    return pl.pallas_call(
        _router_kernel,
        grid=(T // tm,),
        in_specs=[pl.BlockSpec((tm, D), lambda i: (i, 0)),
                  pl.BlockSpec((D, tm), lambda i: (0, 0)),
                  pl.BlockSpec((NQ, D), lambda i: (0, 0)),
                  kspec, kspec, kspec, kspec],
        out_specs=[pl.BlockSpec((D, tm), lambda i: (0, i)),
                   pl.BlockSpec((2 * H, N_KEYS, tm), lambda i: (0, 0, i)),
                   pl.BlockSpec((H, N_KEYS, tm), lambda i: (0, 0, i)),
                   pl.BlockSpec((H, N_KEYS, tm), lambda i: (0, 0, i))],
        out_shape=[jax.ShapeDtypeStruct((D, T), _MXU),
                   jax.ShapeDtypeStruct((2 * H, N_KEYS, T), jnp.float32),
                   jax.ShapeDtypeStruct((H, N_KEYS, T), _GATE),
                   jax.ShapeDtypeStruct((H, N_KEYS, T), _GATE)],
        scratch_shapes=[pltpu.VMEM((NQ, tm), _MXU), pltpu.VMEM((NQ, tm), _MXU)],
        compiler_params=_params(("parallel",)),
        name="peer_router",
    )(x2, gb, wqt, k1h, k1l, k2h, k2l)


def _gelu(a):
    return 0.5 * a * (1.0 + lax.erf(a * (2.0 ** -0.5)))


def _peer_dense_kernel(*refs, n_e, final_norm):
    refs = list(refs)
    hnt_ref, u_ref, vt_ref, row_ref, rank2_ref, e2_ref, x_ref = refs[:7]
    gf_ref = refs[7] if final_norm else None
    out_ref, acc_ref, gt_ref = refs[-3:]
    s = pl.program_id(0)
    n_pairs = pl.num_programs(0) - 1
    eb = u_ref.shape[0]
    sub = eb // N_KEYS

    @pl.when(s == 0)
    def _():
        gt_ref[...] = jnp.zeros_like(gt_ref)
        acc_ref[...] = jnp.zeros_like(acc_ref)

    slot = s % 2
    e_a = jnp.minimum(s, n_pairs - 1) % n_e
    e_y = jnp.clip(s - 1, 0, n_pairs - 1) % n_e

    acc_ref[...] += jnp.dot(vt_ref[...], gt_ref[1 - slot], preferred_element_type=jnp.float32)

    at = jnp.dot(u_ref[...], hnt_ref[...], preferred_element_type=jnp.float32)
    for ii in range(sub):
        i = e_a * sub + ii
        rs = slice(ii * N_KEYS, (ii + 1) * N_KEYS)
        gate = None
        for h in range(PEER_HEADS):
            cnt = row_ref[2 * h, pl.ds(i, 1), :].astype(_GATE)
            e1 = row_ref[2 * h + 1, pl.ds(i, 1), :].astype(_GATE)
            term = jnp.where(rank2_ref[h] < cnt, e2_ref[h], jnp.zeros((), _GATE)) * e1
            gate = term if gate is None else gate + term
        gt_ref[slot, rs, :] = (gate * _gelu(at[rs, :]).astype(_GATE)).astype(gt_ref.dtype)

    @pl.when((e_y == n_e - 1) & (s >= 1))
    def _():
        res = x_ref[...] + acc_ref[...].T
        if final_norm:
            res = _rmsnorm(res, gf_ref[...])
        out_ref[...] = res
        acc_ref[...] = jnp.zeros_like(acc_ref)


def _peer_dense(hnt, u, vt, rowdat, rank2, e2, x2, g_final=None, tm=512, eb=1024):
    T, D = x2.shape
    H = PEER_HEADS
    n_e = N_EXPERTS // eb
    n_pairs = (T // tm) * n_e
    final_norm = g_final is not None

    def pair(s, lag):
        p = jnp.clip(s - lag, 0, n_pairs - 1)
        return p // n_e, p % n_e

    in_specs = [pl.BlockSpec((D, tm), lambda s: (0, pair(s, 0)[0])),
                pl.BlockSpec((eb, D), lambda s: (pair(s, 0)[1], 0)),
                pl.BlockSpec((D, eb), lambda s: (0, pair(s, 1)[1])),
                pl.BlockSpec((2 * H, N_KEYS, tm), lambda s: (0, 0, pair(s, 0)[0])),
                pl.BlockSpec((H, N_KEYS, tm), lambda s: (0, 0, pair(s, 0)[0])),
                pl.BlockSpec((H, N_KEYS, tm), lambda s: (0, 0, pair(s, 0)[0])),
                pl.BlockSpec((tm, D), lambda s: (pair(s, 1)[0], 0))]
    args = [hnt, u, vt, rowdat, rank2, e2, x2]
    if final_norm:
        in_specs.append(pl.BlockSpec((1, D), lambda s: (0, 0)))
        args.append(g_final)
    return pl.pallas_call(
        functools.partial(_peer_dense_kernel, n_e=n_e, final_norm=final_norm),
        grid=(n_pairs + 1,),
        in_specs=in_specs,
        out_specs=pl.BlockSpec((tm, D), lambda s: (pair(s, 1)[0], 0)),
        out_shape=jax.ShapeDtypeStruct((T, D), jnp.float32),
        scratch_shapes=[pltpu.VMEM((D, tm), jnp.float32),
                        pltpu.VMEM((2, eb, tm), _MXU)],
        compiler_params=_params(("arbitrary",)),
        name="peer_dense_final" if final_norm else "peer_dense",
    )(*args)


def _split_hi_lo(w):
    hi = w.astype(_MXU)
    lo = (w - hi.astype(jnp.float32)).astype(_MXU)
    return hi, lo


def _peer(x2, g, wqt, sk1, sk2, u, vt, g_final=None):
    k1h, k1l = _split_hi_lo(sk1)
    k2h, k2l = _split_hi_lo(sk2)
    hnt, rowdat, rank2, e2 = _peer_router(x2, g, wqt, k1h, k1l, k2h, k2l)
    return _peer_dense(hnt, u, vt, rowdat, rank2, e2, x2, g_final)


def _trunk(x, p):
    B, S, D = x.shape
    x2 = x.reshape(B * S, D)
    qkv = _norm_proj(x2, p["ln_mix"][0:1], p["w_qkv_na"], S)
    o = _na_attention(qkv.reshape(B, S, 3 * D), p["na_bias"], B, S)
    x2 = _proj_residual(o.reshape(B * S, D), p["w_o_na"], x2)
    x2 = _peer(x2, p["ln_ffn"][0:1], p["wqt"][0], p["sk1"][0], p["sk2"][0], p["u"][0], p["vt"][0])
    tables = _rope_tables(S)
    outs, lses = [], []
    for gi, (window, dil) in enumerate(DIL_CONFIGS):
        qkv_g = _norm_proj(x2, p["ln_mix"][1:2], p["w_qkv_dil"], S, col0=3 * gi, dil=dil, rope_tables=tables)
        o_g, lse_g = _dil_group(qkv_g, B, S, window, dil)
        outs.append(o_g)
        lses.append(lse_g)
    x2 = _dil_merge_proj(outs, lses, x2, p["w_o_dil"])
    x2 = _peer(x2, p["ln_ffn"][1:2], p["wqt"][1], p["sk1"][1], p["sk2"][1], p["u"][1], p["vt"][1],
               g_final=p["ln_final"])
    return x2.reshape(B, S, D)


def kernel(x_prompt, x_sample, ln_mix, ln_ffn, ln_final, w_qkv_na, rpb_na, w_o_na, w_qkv_dil, w_o_dil,
           w_query_peer, subkeys1_peer, subkeys2_peer, u_peer, v_peer):
    p = {
        "ln_mix": ln_mix, "ln_ffn": ln_ffn, "ln_final": ln_final.reshape(1, -1),
        "w_qkv_na": w_qkv_na[0].astype(_MXU), "w_o_na": w_o_na[0].astype(_MXU),
        "na_bias": _na_bias_table(rpb_na[0]),
        "w_qkv_dil": w_qkv_dil[0].astype(_MXU), "w_o_dil": w_o_dil[0].astype(_MXU),
        "wqt": jnp.swapaxes(w_query_peer, 1, 2).astype(_MXU),
        "sk1": subkeys1_peer, "sk2": subkeys2_peer,
        "u": u_peer.astype(_MXU), "vt": jnp.swapaxes(v_peer, 1, 2).astype(_MXU),
    }
    return _trunk(x_prompt, p), _trunk(x_sample, p)
```

```python
import functools

import numpy as np
import jax
import jax.numpy as jnp
from jax import lax
from jax.experimental import pallas as pl
from jax.experimental.pallas import tpu as pltpu

D_MODEL = 1024
HEAD_DIM = 64
N_HEADS = 16
GRID_W = 64
NA_KH = 8
NA_KW = 16
NA_ROWS_PER_STEP = 4
NA_HEAD_LANES = 512
DIL_CONFIGS = ((128, 1), (512, 4), (2048, 16))
DIL_BLOCK = 256
DIL_HALO = 64
ROPE_THETA = 10000.0
PEER_HEADS = 8
N_KEYS = 128
N_EXPERTS = N_KEYS * N_KEYS
PEER_TOPK = 16
RMS_EPS = 1e-6
NEG_INF = -1e30
LANES = 128

_MXU = jnp.bfloat16
_GATE = jnp.bfloat16
_VMEM_LIMIT = 56 * 1024 * 1024

_NT = (((1,), (1,)), ((), ()))

_MARK = -(2.0 ** 127)
_MARK_STEP = 2.0 ** 104
_MARK_LIMIT = -(2.0 ** 126)
_SCORE_FLOOR = -(2.0 ** 120)


def _params(sem):
    return pltpu.CompilerParams(dimension_semantics=sem, vmem_limit_bytes=_VMEM_LIMIT)


def _rmsnorm(x, g):
    ms = jnp.mean(x * x, axis=-1, keepdims=True)
    return x * lax.rsqrt(ms + RMS_EPS) * g


def _norm_proj_kernel(*refs, rope, dil):
    refs = list(refs)
    x_ref, g_ref, w_ref = refs[:3]
    cos_ref, sin_ref = (refs[3], refs[4]) if rope else (None, None)
    o_ref, xn_ref = refs[5:7] if rope else refs[3:5]
    y_ref = refs[-1] if (rope or dil > 1) else None
    j = pl.program_id(1)

    @pl.when(j == 0)
    def _():
        xn_ref[...] = _rmsnorm(x_ref[...], g_ref[...]).astype(xn_ref.dtype)

    acc = jnp.dot(xn_ref[...], w_ref[...], preferred_element_type=jnp.float32)
    scale = jnp.where(j == 0, HEAD_DIM ** -0.5, 1.0).astype(jnp.float32)

    if y_ref is None:
        o_ref[...] = (acc * scale).astype(o_ref.dtype)
        return

    n_lt = acc.shape[1] // LANES
    if rope:
        @pl.when(j == 2)
        def _():
            for t in range(n_lt):
                y_ref[t] = acc[:, t * LANES:(t + 1) * LANES]

        @pl.when(j != 2)
        def _():
            cos = cos_ref[...]
            sin = sin_ref[...]
            lane = lax.broadcasted_iota(jnp.int32, cos.shape, 1)
            first_half = (lane % HEAD_DIM) < HEAD_DIM // 2
            for t in range(n_lt):
                xt = acc[:, t * LANES:(t + 1) * LANES]
                partner = jnp.where(first_half,
                                    pltpu.roll(xt, LANES - HEAD_DIM // 2, 1),
                                    pltpu.roll(xt, HEAD_DIM // 2, 1))
                y_ref[t] = (xt * cos + partner * sin) * scale
    else:
        for t in range(n_lt):
            y_ref[t] = acc[:, t * LANES:(t + 1) * LANES] * scale

    rows = acc.shape[0] // dil
    for r in range(dil):
        for t in range(n_lt):
            src = y_ref[t] if dil == 1 else y_ref[t, pl.ds(r, rows, stride=dil), :]
            c0 = r * D_MODEL + t * LANES
            o_ref[:, c0:c0 + LANES] = src.astype(o_ref.dtype)


def _norm_proj(x2, g, w, seq_len, col0=0, dil=1, rope_tables=None, tm=512):
    T, D = x2.shape
    rope = rope_tables is not None
    in_specs = [
        pl.BlockSpec((tm, D), lambda i, j: (i, 0)),
        pl.BlockSpec((1, D), lambda i, j: (0, 0)),
        pl.BlockSpec((D, D), lambda i, j: (0, col0 + j)),
    ]
    args = [x2, g, w]
    if rope:
        nper = seq_len // tm
        tab = pl.BlockSpec((tm, LANES), lambda i, j: (i % nper, 0))
        in_specs += [tab, tab]
        args += list(rope_tables)
    scratch = [pltpu.VMEM((tm, D), _MXU)]
    if rope or dil > 1:
        scratch.append(pltpu.VMEM((D // LANES, tm, LANES), jnp.float32))
    return pl.pallas_call(
        functools.partial(_norm_proj_kernel, rope=rope, dil=dil),
        grid=(T // tm, 3),
        in_specs=in_specs,
        out_specs=pl.BlockSpec((tm // dil, dil * D), lambda i, j: (i, j)),
        out_shape=jax.ShapeDtypeStruct((T // dil, 3 * dil * D), _MXU),
        scratch_shapes=scratch,
        compiler_params=_params(("parallel", "arbitrary")),
        name=f"norm_proj_rope_d{dil}" if rope else "norm_proj",
    )(*args)


def _rope_tables(seq_len):
    half = HEAD_DIM // 2
    freqs = ROPE_THETA ** (-jnp.arange(half, dtype=jnp.float32) / half)
    ang = jnp.arange(seq_len, dtype=jnp.float32)[:, None] * freqs[None, :]
    cos = jnp.cos(ang)
    sin = jnp.sin(ang)
    cos_t = jnp.tile(cos, (1, LANES // half))
    sin_t = jnp.tile(jnp.concatenate([-sin, sin], axis=1), (1, LANES // HEAD_DIM))
    return cos_t, sin_t


def _na_kernel(q_ref, k0_ref, k1_ref, k2_ref, v0_ref, v1_ref, v2_ref, bias_ref, o_ref,
               kbuf, vbuf, s_scr, m_scr, e_scr, *, rows):
    g = pl.program_id(2)
    rp = NA_ROWS_PER_STEP
    blk = rp * GRID_W
    nblk = rows // rp
    b0 = jnp.clip(g - 1, 0, nblk - 3)
    for i, (kr, vr) in enumerate(((k0_ref, v0_ref), (k1_ref, v1_ref), (k2_ref, v2_ref))):
        kbuf[i * blk:(i + 1) * blk, :] = kr[0]
        vbuf[i * blk:(i + 1) * blk, :] = vr[0]
    nkeys = NA_KH * GRID_W
    ones = jnp.ones((nkeys, LANES), kbuf.dtype)
    nh = q_ref.shape[2] // HEAD_DIM
    offs = []
    for r in range(rp):
        r_abs = g * rp + r
        rs = jnp.clip(r_abs - NA_KH // 2, 0, rows - NA_KH)
        off = pl.multiple_of((rs - b0 * rp) * GRID_W, GRID_W)
        offs.append(off)
        var = rs - r_abs + NA_KH - 1
        for h in range(nh):
            hs = slice(h * HEAD_DIM, (h + 1) * HEAD_DIM)
            qh = q_ref[0, r * GRID_W:(r + 1) * GRID_W, hs]
            s = lax.dot_general(qh, kbuf[pl.ds(off, nkeys), hs], _NT, preferred_element_type=jnp.float32)
            s = s + bias_ref[var, h]
            s_scr[r * nh + h] = s
            m_scr[r * nh + h] = jnp.broadcast_to(jnp.max(s, axis=-1, keepdims=True), (GRID_W, LANES))
    for u in range(rp * nh):
        mb = m_scr[u]
        e_scr[u] = jnp.exp(s_scr[u] - jnp.concatenate([mb] * (nkeys // LANES), axis=1)).astype(e_scr.dtype)
    for r in range(rp):
        for h in range(nh):
            hs = slice(h * HEAD_DIM, (h + 1) * HEAD_DIM)
            e = e_scr[r * nh + h]
            l = jnp.dot(e, ones, preferred_element_type=jnp.float32)
            o = jnp.dot(e, vbuf[pl.ds(offs[r], nkeys), hs], preferred_element_type=jnp.float32)
            o_ref[0, r * GRID_W:(r + 1) * GRID_W, hs] = (o / l[:, :HEAD_DIM]).astype(o_ref.dtype)


def _na_bias_table(rpb):
    qc = np.arange(GRID_W)[:, None]
    kc = np.arange(GRID_W)[None, :]
    cs = np.clip(qc - NA_KW // 2, 0, GRID_W - NA_KW)
    ok = (kc >= cs) & (kc < cs + NA_KW)
    col_neg = jnp.asarray(np.where(ok, 0.0, NEG_INF), jnp.float32)
    col_off = np.clip(kc - qc + NA_KW - 1, 0, 2 * NA_KW - 2)
    tab = rpb.astype(jnp.float32)[:, :, col_off] + col_neg[None, None]
    variants = []
    for var in range(NA_KH):
        t = tab[:, var:var + NA_KH]
        variants.append(jnp.transpose(t, (0, 2, 1, 3)).reshape(N_HEADS, GRID_W, NA_KH * GRID_W))
    return jnp.stack(variants, axis=0)


def _na_attention(qkv, bias_tab, B, S):
    rows = S // GRID_W
    rp = NA_ROWS_PER_STEP
    blk = rp * GRID_W
    nblk = rows // rp
    wl = NA_HEAD_LANES
    nhp = D_MODEL // wl
    hp_heads = wl // HEAD_DIM

    def kv_spec(which, i):
        return pl.BlockSpec(
            (1, blk, wl),
            lambda hp, b, g: (b, jnp.clip(g - 1, 0, nblk - 3) + i, which * nhp + hp))

    return pl.pallas_call(
        functools.partial(_na_kernel, rows=rows),
        grid=(nhp, B, nblk),
        in_specs=[pl.BlockSpec((1, blk, wl), lambda hp, b, g: (b, g, hp))]
        + [kv_spec(1, i) for i in range(3)] + [kv_spec(2, i) for i in range(3)]
        + [pl.BlockSpec((NA_KH, hp_heads, GRID_W, NA_KH * GRID_W), lambda hp, b, g: (0, hp, 0, 0))],
        out_specs=pl.BlockSpec((1, blk, wl), lambda hp, b, g: (b, g, hp)),
        out_shape=jax.ShapeDtypeStruct((B, S, D_MODEL), qkv.dtype),
        scratch_shapes=[pltpu.VMEM((3 * blk, wl), qkv.dtype),
                        pltpu.VMEM((3 * blk, wl), qkv.dtype),
                        pltpu.VMEM((rp * hp_heads, GRID_W, NA_KH * GRID_W), jnp.float32),
                        pltpu.VMEM((rp * hp_heads, GRID_W, LANES), jnp.float32),
                        pltpu.VMEM((rp * hp_heads, GRID_W, NA_KH * GRID_W), qkv.dtype)],
        compiler_params=_params(("arbitrary", "arbitrary", "arbitrary")),
        name="na_attention",
    )(qkv, qkv, qkv, qkv, qkv, qkv, qkv, bias_tab)


def _proj_res_kernel(o_ref, w_ref, x_ref, out_ref):
    out_ref[...] = x_ref[...] + jnp.dot(o_ref[...], w_ref[...], preferred_element_type=jnp.float32)


def _proj_residual(o2, w, x2, tm=512):
    T, D = x2.shape
    return pl.pallas_call(
        _proj_res_kernel,
        grid=(T // tm,),
        in_specs=[pl.BlockSpec((tm, D), lambda i: (i, 0)),
                  pl.BlockSpec((D, D), lambda i: (0, 0)),
                  pl.BlockSpec((tm, D), lambda i: (i, 0))],
        out_specs=pl.BlockSpec((tm, D), lambda i: (i, 0)),
        out_shape=jax.ShapeDtypeStruct((T, D), jnp.float32),
        compiler_params=_params(("parallel",)),
        name="proj_residual",
    )(o2, w, x2)


def _dil_kernel(q_ref, kp_ref, kc_ref, kn_ref, vp_ref, vc_ref, vn_ref, o_ref, lse_ref, kbuf, vbuf,
                s_scr, m_scr, e_scr, *,
                n_blocks, half):
    jb = pl.program_id(2)
    C = q_ref.shape[1]
    hc = kp_ref.shape[1]
    nk = C + 2 * hc
    for buf, (p_ref, c_ref, n_ref) in ((kbuf, (kp_ref, kc_ref, kn_ref)), (vbuf, (vp_ref, vc_ref, vn_ref))):
        buf[0:hc, :] = p_ref[0]
        buf[hc:hc + C, :] = c_ref[0]
        buf[hc + C:nk, :] = n_ref[0]
    qi = lax.broadcasted_iota(jnp.int32, (C, nk), 0)
    ci = lax.broadcasted_iota(jnp.int32, (C, nk), 1)
    lo = jnp.where(jb == 0, hc, 0)
    hi = jnp.where(jb == n_blocks - 1, hc + C, nk)
    mask = (jnp.abs(ci - hc - qi) <= half) & (ci >= lo) & (ci < hi)
    ones = jnp.ones((nk, LANES), kbuf.dtype)
    lane = lax.broadcasted_iota(jnp.int32, (C, LANES), 1)
    lse_all = jnp.zeros((C, LANES), jnp.float32)
    for h in range(N_HEADS):
        hs = slice(h * HEAD_DIM, (h + 1) * HEAD_DIM)
        s = lax.dot_general(q_ref[0, :, hs], kbuf[:, hs], _NT, preferred_element_type=jnp.float32)
        s = jnp.where(mask, s, NEG_INF)
        s_scr[h] = s
        mt = s[:, :LANES]
        for t in range(1, nk // LANES):
            mt = jnp.maximum(mt, s[:, t * LANES:(t + 1) * LANES])
        m_scr[h] = jnp.broadcast_to(jnp.max(mt, axis=-1, keepdims=True), (C, LANES))
    for h in range(N_HEADS):
        mb = m_scr[h]
        e_scr[h] = jnp.exp(s_scr[h] - jnp.concatenate([mb] * (nk // LANES), axis=1)).astype(e_scr.dtype)
    for h in range(N_HEADS):
        hs = slice(h * HEAD_DIM, (h + 1) * HEAD_DIM)
        e = e_scr[h]
        l = jnp.dot(e, ones, preferred_element_type=jnp.float32)
        o = jnp.dot(e, vbuf[:, hs], preferred_element_type=jnp.float32)
        o_ref[0, :, hs] = (o / l[:, :HEAD_DIM]).astype(o_ref.dtype)
        lse_all = jnp.where(lane == h, m_scr[h] + jnp.log(l), lse_all)
    lse_ref[0] = lse_all


def _dil_group(qkv_g, B, S, window, dil):
    n = S // dil
    C = DIL_BLOCK
    hc = DIL_HALO
    nb = n // C
    half = window // (2 * dil)
    assert half <= hc and n % C == 0 and C % hc == 0
    per = C // hc
    qkv_v = qkv_g.reshape(B, n, qkv_g.shape[-1])

    def cur(which):
        return pl.BlockSpec((1, C, D_MODEL), lambda b, r, jb: (b, jb, which * dil + r))

    def edge(which, shift):
        def idx(b, r, jb):
            return (b, jnp.clip(per * jb + shift, 0, per * nb - 1), which * dil + r)
        return pl.BlockSpec((1, hc, D_MODEL), idx)

    o, lse = pl.pallas_call(
        functools.partial(_dil_kernel, n_blocks=nb, half=half),
        grid=(B, dil, nb),
        in_specs=[cur(0), edge(1, -1), cur(1), edge(1, per), edge(2, -1), cur(2), edge(2, per)],
        out_specs=[pl.BlockSpec((1, C, D_MODEL), lambda b, r, jb: (b, jb, r)),
                   pl.BlockSpec((1, C, LANES), lambda b, r, jb: (b, jb, r))],
        out_shape=[jax.ShapeDtypeStruct((B, n, dil * D_MODEL), jnp.float32),
                   jax.ShapeDtypeStruct((B, n, dil * LANES), jnp.float32)],
        scratch_shapes=[pltpu.VMEM((C + 2 * hc, D_MODEL), qkv_g.dtype),
                        pltpu.VMEM((C + 2 * hc, D_MODEL), qkv_g.dtype),
                        pltpu.VMEM((N_HEADS, C, C + 2 * hc), jnp.float32),
                        pltpu.VMEM((N_HEADS, C, LANES), jnp.float32),
                        pltpu.VMEM((N_HEADS, C, C + 2 * hc), qkv_g.dtype)],
        compiler_params=_params(("parallel", "parallel", "arbitrary")),
        name=f"dil_group_d{dil}",
    )(qkv_v, qkv_v, qkv_v, qkv_v, qkv_v, qkv_v, qkv_v)
    return o.reshape(B * n, dil * D_MODEL), lse.reshape(B * n, dil * LANES)


def _dil_merge_kernel(o1_ref, o2_ref, o3_ref, l1_ref, l2_ref, l3_ref, x_ref, w_ref, exp_ref, out_ref,
                      o_scr, l_scr):
    tm = x_ref.shape[0]
    outs, lses = [], []
    for gi, (o_ref, l_ref) in enumerate(((o1_ref, l1_ref), (o2_ref, l2_ref), (o3_ref, l3_ref))):
        dil = DIL_CONFIGS[gi][1]
        if dil == 1:
            outs.append(o_ref[...])
            lses.append(l_ref[...])
            continue
        rows = tm // dil
        n_lt = D_MODEL // LANES
        for r in range(dil):
            for t in range(n_lt):
                c0 = r * D_MODEL + t * LANES
                o_scr[gi, t, pl.ds(r, rows, stride=dil), :] = o_ref[:, c0:c0 + LANES]
            l_scr[gi, pl.ds(r, rows, stride=dil), :] = l_ref[:, r * LANES:(r + 1) * LANES]
        outs.append(jnp.concatenate([o_scr[gi, t] for t in range(n_lt)], axis=1))
        lses.append(l_scr[gi])
    m = jnp.maximum(jnp.maximum(lses[0], lses[1]), lses[2])
    es = [jnp.exp(l - m) for l in lses]
    z = es[0] + es[1] + es[2]
    merged = None
    for e, o in zip(es, outs):
        w = e / z
        hi = w.astype(_MXU)
        lo = (w - hi.astype(jnp.float32)).astype(_MXU)
        wide = (jnp.dot(hi, exp_ref[...], preferred_element_type=jnp.float32)
                + jnp.dot(lo, exp_ref[...], preferred_element_type=jnp.float32))
        term = wide * o
        merged = term if merged is None else merged + term
    out_ref[...] = x_ref[...] + jnp.dot(merged.astype(_MXU), w_ref[...],
                                        preferred_element_type=jnp.float32)


def _dil_merge_proj(outs, lses, x2, w, tm=256):
    T, D = x2.shape
    expand = np.zeros((LANES, D), np.float32)
    for h in range(N_HEADS):
        expand[h, h * HEAD_DIM:(h + 1) * HEAD_DIM] = 1.0
    expand = jnp.asarray(expand, _MXU)
    row = pl.BlockSpec((tm, D), lambda i: (i, 0))
    o_specs = [pl.BlockSpec((tm // d, d * D), lambda i: (i, 0)) for _, d in DIL_CONFIGS]
    l_specs = [pl.BlockSpec((tm // d, d * LANES), lambda i: (i, 0)) for _, d in DIL_CONFIGS]
    ng = len(DIL_CONFIGS)
    return pl.pallas_call(
        _dil_merge_kernel,
        grid=(T // tm,),
        in_specs=o_specs + l_specs + [row, pl.BlockSpec((D, D), lambda i: (0, 0)),
                                      pl.BlockSpec((LANES, D), lambda i: (0, 0))],
        out_specs=row,
        out_shape=jax.ShapeDtypeStruct((T, D), jnp.float32),
        scratch_shapes=[pltpu.VMEM((ng, D // LANES, tm, LANES), jnp.float32),
                        pltpu.VMEM((ng, tm, LANES), jnp.float32)],
        compiler_params=_params(("parallel",)),
        name="dil_merge_proj",
    )(*outs, *lses, x2, w, expand)


def _top16_rows(s, want_rank):
    K, L = s.shape
    kio = lax.broadcasted_iota(jnp.int32, (K, L), 0).astype(jnp.float32)
    aio = lax.broadcasted_iota(jnp.int32, (PEER_TOPK, L), 0)
    vals = jnp.zeros((PEER_TOPK, L), jnp.float32)
    idxs = jnp.zeros((PEER_TOPK, L), jnp.float32)
    rank = jnp.full((K, L), float(PEER_TOPK), jnp.float32) if want_rank else None
    for a in range(PEER_TOPK):
        m = jnp.max(s, axis=0, keepdims=True)
        idx = jnp.min(jnp.where(s == m, kio, float(K)), axis=0, keepdims=True)
        sel = kio == idx
        if want_rank:
            rank = jnp.where(sel, float(a), rank)
        s = jnp.where(sel, -jnp.inf, s)
        vals = jnp.where(aio == a, m, vals)
        idxs = jnp.where(aio == a, idx, idxs)
    return vals, idxs, rank


def _pair_candidates(v1, v2):
    L = v1.shape[1]
    K = PEER_TOPK
    blocks = [v1 + v2[0:1]]
    for b in range(1, 8):
        blocks.append(v1[0:8] + v2[b:b + 1])
    blocks.append(v1[0:1] + v2[8:16])
    cand = jnp.concatenate(blocks, axis=0)
    R = cand.shape[0]
    r = lax.broadcasted_iota(jnp.int32, (R, L), 0)
    a_mid = (r - 16) & 7
    b_mid = ((r - 16) >> 3) + 1
    a_of = jnp.where(r < 16, r, jnp.where(r < 72, a_mid, 0))
    b_of = jnp.where(r < 16, 0, jnp.where(r < 72, b_mid, r - 64))
    flat = (a_of * K + b_of).astype(jnp.float32)
    cand = jnp.where((a_of + 1) * (b_of + 1) <= K, cand, -jnp.inf)
    return cand, flat, a_of.astype(jnp.float32)


def _pair_counts(v1, v2):
    L = v1.shape[1]
    K = PEER_TOPK
    cand, flat, a_of_f = _pair_candidates(v1, v2)
    aio = lax.broadcasted_iota(jnp.int32, (K, L), 0).astype(jnp.float32)
    cnt = jnp.zeros((K, L), jnp.float32)
    top = v1[0:1] + v2[0:1]
    z = jnp.zeros((1, L), jnp.float32)
    for _ in range(K):
        m = jnp.max(cand, axis=0, keepdims=True)
        f = jnp.min(jnp.where(cand == m, flat, float(K * K)), axis=0, keepdims=True)
        sel = flat == f
        a_sel = jnp.max(jnp.where(sel, a_of_f, 0.0), axis=0, keepdims=True)
        cand = jnp.where(sel, -jnp.inf, cand)
        cnt = cnt + jnp.where(aio == a_sel, 1.0, 0.0)
        z = z + jnp.exp(m - top)
    return cnt, z


def _route_head(s1, s2):
    v1, i1, _ = _top16_rows(s1, False)
    v2, _, rank2 = _top16_rows(s2, True)
    cnt, z = _pair_counts(v1, v2)
    kio = lax.broadcasted_iota(jnp.int32, s1.shape, 0).astype(jnp.float32)
    cnt_i = jnp.zeros_like(s1)
    for a in range(PEER_TOPK):
        cnt_i = jnp.where(kio == i1[a:a + 1], cnt[a:a + 1], cnt_i)
    e1 = jnp.exp(s1 - v1[0:1])
    e2 = jnp.exp(s2 - v2[0:1]) / z
    return cnt_i, e1, rank2, e2


def _mark_top16(s):
    L = s.shape[1]
    aio = lax.broadcasted_iota(jnp.int32, (PEER_TOPK, L), 0)
    vals = jnp.zeros((PEER_TOPK, L), jnp.float32)
    for a in range(PEER_TOPK):
        m = jnp.max(s, axis=0, keepdims=True)
        s = jnp.where(s == m, _MARK + a * _MARK_STEP, s)
        vals = jnp.where(aio == a, m, vals)
    marked = s < _MARK_LIMIT
    rank = jnp.where(marked, (s - _MARK) * (1.0 / _MARK_STEP), float(PEER_TOPK))
    n_marked = jnp.sum(jnp.where(marked, 1.0, 0.0), axis=0, keepdims=True)
    return vals, rank, n_marked


def _pair_counts_distinct(v1, v2):
    L = v1.shape[1]
    K = PEER_TOPK
    orig, _, _ = _pair_candidates(v1, v2)
    cand = orig
    for _ in range(K):
        m = jnp.max(cand, axis=0, keepdims=True)
        cand = jnp.where(cand == m, _MARK, cand)
    sel = cand == _MARK
    self = jnp.where(sel, 1.0, 0.0)
    c8 = self[16:24]
    for b in range(2, 8):
        c8 = c8 + self[8 + 8 * b:16 + 8 * b]
    row8 = lax.broadcasted_iota(jnp.int32, (8, L), 0)
    c8 = c8 + jnp.where(row8 == 0, jnp.sum(self[72:80], axis=0, keepdims=True), 0.0)
    cnt = self[0:16] + jnp.concatenate([c8, jnp.zeros((8, L), jnp.float32)], axis=0)
    top = v1[0:1] + v2[0:1]
    z = jnp.sum(jnp.where(sel, jnp.exp(orig - top), 0.0), axis=0, keepdims=True)
    return cnt, z, jnp.sum(self, axis=0, keepdims=True)


def _route_head_distinct(s1, s2):
    v1, rank1, n1 = _mark_top16(s1)
    v2, rank2, n2 = _mark_top16(s2)
    cnt, z, n3 = _pair_counts_distinct(v1, v2)
    cnt_i = jnp.zeros_like(s1)
    for a in range(PEER_TOPK):
        cnt_i = jnp.where(rank1 == float(a), cnt[a:a + 1], cnt_i)
    e1 = jnp.exp(s1 - v1[0:1])
    e2 = jnp.exp(s2 - v2[0:1]) / z
    k = float(PEER_TOPK)
    floor = jnp.minimum(jnp.min(s1, axis=0, keepdims=True), jnp.min(s2, axis=0, keepdims=True))
    ok = (n1 == k) & (n2 == k) & (n3 == k) & (floor > _SCORE_FLOOR)
    return (cnt_i, e1, rank2, e2), ok


def _router_kernel(x_ref, gb_ref, wqt_ref, k1h_ref, k1l_ref, k2h_ref, k2l_ref,
                   hnt_ref, row_ref, rank2_ref, e2_ref, qh_scr, ql_scr):
    xt = x_ref[...].T
    ms = jnp.mean(xt * xt, axis=0, keepdims=True)
    hnt = (xt * lax.rsqrt(ms + RMS_EPS) * gb_ref[...]).astype(hnt_ref.dtype)
    hnt_ref[...] = hnt
    qt = jnp.dot(wqt_ref[...], hnt, preferred_element_type=jnp.float32)
    q_hi = qt.astype(_MXU)
    qh_scr[...] = q_hi
    ql_scr[...] = (qt - q_hi.astype(jnp.float32)).astype(_MXU)
    L = xt.shape[1]

    heads_per_iter = 4

    def scores(kh_ref, kl_ref, h, r0, ls):
        qh = qh_scr[pl.ds(r0, N_KEYS), ls]
        ql = ql_scr[pl.ds(r0, N_KEYS), ls]
        return (jnp.dot(kh_ref[h], qh, preferred_element_type=jnp.float32)
                + jnp.dot(kh_ref[h], ql, preferred_element_type=jnp.float32)
                + jnp.dot(kl_ref[h], qh, preferred_element_type=jnp.float32))

    def store(res, h, ls):
        cnt_i, e1, rank2, e2 = res
        row_ref[2 * h, :, ls] = cnt_i
        row_ref[2 * h + 1, :, ls] = e1
        rank2_ref[h, :, ls] = rank2.astype(rank2_ref.dtype)
        e2_ref[h, :, ls] = e2.astype(e2_ref.dtype)

    def head_group(hg, carry):
        units = []
        for hh in range(heads_per_iter):
            h = hg * heads_per_iter + hh
            r1 = pl.multiple_of(h * 2 * N_KEYS, 2 * N_KEYS)
            for lt in range(L // LANES):
                ls = slice(lt * LANES, (lt + 1) * LANES)
                s1 = scores(k1h_ref, k1l_ref, h, r1, ls)
                s2 = scores(k2h_ref, k2l_ref, h, r1 + N_KEYS, ls)
                res, ok = _route_head_distinct(s1, s2)
                store(res, h, ls)
                units.append((h, ls, s1, s2, jnp.sum(jnp.where(ok, 0.0, 1.0))))
        for h, ls, s1, s2, n_bad in units:
            @pl.when(n_bad > 0.0)
            def _():
                store(_route_head(s1, s2), h, ls)
        return carry

    lax.fori_loop(0, PEER_HEADS // heads_per_iter, head_group, 0)


def _peer_router(x2, g, wqt, k1h, k1l, k2h, k2l, tm=256):
    T, D = x2.shape
    NQ = wqt.shape[0]
    H = PEER_HEADS
    gb = jnp.broadcast_to(g.reshape(D, 1), (D, tm))
    kspec = pl.BlockSpec((H, N_KEYS, N_KEYS), lambda i: (0, 0, 0))
    return pl.pallas_call(
        _router_kernel,
        grid=(T // tm,),
        in_specs=[pl.BlockSpec((tm, D), lambda i: (i, 0)),
                  pl.BlockSpec((D, tm), lambda i: (0, 0)),
                  pl.BlockSpec((NQ, D), lambda i: (0, 0)),
                  kspec, kspec, kspec, kspec],
        out_specs=[pl.BlockSpec((D, tm), lambda i: (0, i)),
                   pl.BlockSpec((2 * H, N_KEYS, tm), lambda i: (0, 0, i)),
                   pl.BlockSpec((H, N_KEYS, tm), lambda i: (0, 0, i)),
                   pl.BlockSpec((H, N_KEYS, tm), lambda i: (0, 0, i))],
        out_shape=[jax.ShapeDtypeStruct((D, T), _MXU),
                   jax.ShapeDtypeStruct((2 * H, N_KEYS, T), jnp.float32),
                   jax.ShapeDtypeStruct((H, N_KEYS, T), _GATE),
                   jax.ShapeDtypeStruct((H, N_KEYS, T), _GATE)],
        scratch_shapes=[pltpu.VMEM((NQ, tm), _MXU), pltpu.VMEM((NQ, tm), _MXU)],
        compiler_params=_params(("parallel",)),
        name="peer_router",
    )(x2, gb, wqt, k1h, k1l, k2h, k2l)


def _gelu(a):
    return 0.5 * a * (1.0 + lax.erf(a * (2.0 ** -0.5)))


def _peer_dense_kernel(*refs, n_e, final_norm):
    refs = list(refs)
    hnt_ref, u_ref, vt_ref, row_ref, rank2_ref, e2_ref, x_ref = refs[:7]
    gf_ref = refs[7] if final_norm else None
    out_ref, acc_ref, gt_ref = refs[-3:]
    s = pl.program_id(0)
    n_pairs = pl.num_programs(0) - 1
    eb = u_ref.shape[0]
    sub = eb // N_KEYS

    @pl.when(s == 0)
    def _():
        gt_ref[...] = jnp.zeros_like(gt_ref)
        acc_ref[...] = jnp.zeros_like(acc_ref)

    slot = s % 2
    e_a = jnp.minimum(s, n_pairs - 1) % n_e
    e_y = jnp.clip(s - 1, 0, n_pairs - 1) % n_e

    acc_ref[...] += jnp.dot(vt_ref[...], gt_ref[1 - slot], preferred_element_type=jnp.float32)

    at = jnp.dot(u_ref[...], hnt_ref[...], preferred_element_type=jnp.float32)
    for ii in range(sub):
        i = e_a * sub + ii
        rs = slice(ii * N_KEYS, (ii + 1) * N_KEYS)
        gate = None
        for h in range(PEER_HEADS):
            cnt = row_ref[2 * h, pl.ds(i, 1), :].astype(_GATE)
            e1 = row_ref[2 * h + 1, pl.ds(i, 1), :].astype(_GATE)
            term = jnp.where(rank2_ref[h] < cnt, e2_ref[h], jnp.zeros((), _GATE)) * e1
            gate = term if gate is None else gate + term
        gt_ref[slot, rs, :] = (gate * _gelu(at[rs, :]).astype(_GATE)).astype(gt_ref.dtype)

    @pl.when((e_y == n_e - 1) & (s >= 1))
    def _():
        res = x_ref[...] + acc_ref[...].T
        if final_norm:
            res = _rmsnorm(res, gf_ref[...])
        out_ref[...] = res
        acc_ref[...] = jnp.zeros_like(acc_ref)


def _peer_dense(hnt, u, vt, rowdat, rank2, e2, x2, g_final=None, tm=512, eb=2048):
    T, D = x2.shape
    H = PEER_HEADS
    n_e = N_EXPERTS // eb
    n_pairs = (T // tm) * n_e
    final_norm = g_final is not None

    def pair(s, lag):
        p = jnp.clip(s - lag, 0, n_pairs - 1)
        return p // n_e, p % n_e

    in_specs = [pl.BlockSpec((D, tm), lambda s: (0, pair(s, 0)[0])),
                pl.BlockSpec((eb, D), lambda s: (pair(s, 0)[1], 0)),
                pl.BlockSpec((D, eb), lambda s: (0, pair(s, 1)[1])),
                pl.BlockSpec((2 * H, N_KEYS, tm), lambda s: (0, 0, pair(s, 0)[0])),
                pl.BlockSpec((H, N_KEYS, tm), lambda s: (0, 0, pair(s, 0)[0])),
                pl.BlockSpec((H, N_KEYS, tm), lambda s: (0, 0, pair(s, 0)[0])),
                pl.BlockSpec((tm, D), lambda s: (pair(s, 1)[0], 0))]
    args = [hnt, u, vt, rowdat, rank2, e2, x2]
    if final_norm:
        in_specs.append(pl.BlockSpec((1, D), lambda s: (0, 0)))
        args.append(g_final)
    return pl.pallas_call(
        functools.partial(_peer_dense_kernel, n_e=n_e, final_norm=final_norm),
        grid=(n_pairs + 1,),
        in_specs=in_specs,
        out_specs=pl.BlockSpec((tm, D), lambda s: (pair(s, 1)[0], 0)),
        out_shape=jax.ShapeDtypeStruct((T, D), jnp.float32),
        scratch_shapes=[pltpu.VMEM((D, tm), jnp.float32),
                        pltpu.VMEM((2, eb, tm), _MXU)],
        compiler_params=_params(("arbitrary",)),
        name="peer_dense_final" if final_norm else "peer_dense",
    )(*args)


def _split_hi_lo(w):
    hi = w.astype(_MXU)
    lo = (w - hi.astype(jnp.float32)).astype(_MXU)
    return hi, lo


def _peer(x2, g, wqt, sk1, sk2, u, vt, g_final=None):
    k1h, k1l = _split_hi_lo(sk1)
    k2h, k2l = _split_hi_lo(sk2)
    hnt, rowdat, rank2, e2 = _peer_router(x2, g, wqt, k1h, k1l, k2h, k2l)
    return _peer_dense(hnt, u, vt, rowdat, rank2, e2, x2, g_final)


def _trunk(x, p):
    B, S, D = x.shape
    x2 = x.reshape(B * S, D)
    qkv = _norm_proj(x2, p["ln_mix"][0:1], p["w_qkv_na"], S)
    o = _na_attention(qkv.reshape(B, S, 3 * D), p["na_bias"], B, S)
    x2 = _proj_residual(o.reshape(B * S, D), p["w_o_na"], x2)
    x2 = _peer(x2, p["ln_ffn"][0:1], p["wqt"][0], p["sk1"][0], p["sk2"][0], p["u"][0], p["vt"][0])
    tables = _rope_tables(S)
    outs, lses = [], []
    for gi, (window, dil) in enumerate(DIL_CONFIGS):
        qkv_g = _norm_proj(x2, p["ln_mix"][1:2], p["w_qkv_dil"], S, col0=3 * gi, dil=dil, rope_tables=tables)
        o_g, lse_g = _dil_group(qkv_g, B, S, window, dil)
        outs.append(o_g)
        lses.append(lse_g)
    x2 = _dil_merge_proj(outs, lses, x2, p["w_o_dil"])
    x2 = _peer(x2, p["ln_ffn"][1:2], p["wqt"][1], p["sk1"][1], p["sk2"][1], p["u"][1], p["vt"][1],
               g_final=p["ln_final"])
    return x2.reshape(B, S, D)


def kernel(x_prompt, x_sample, ln_mix, ln_ffn, ln_final, w_qkv_na, rpb_na, w_o_na, w_qkv_dil, w_o_dil,
           w_query_peer, subkeys1_peer, subkeys2_peer, u_peer, v_peer):
    p = {
        "ln_mix": ln_mix, "ln_ffn": ln_ffn, "ln_final": ln_final.reshape(1, -1),
        "w_qkv_na": w_qkv_na[0].astype(_MXU), "w_o_na": w_o_na[0].astype(_MXU),
        "na_bias": _na_bias_table(rpb_na[0]),
        "w_qkv_dil": w_qkv_dil[0].astype(_MXU), "w_o_dil": w_o_dil[0].astype(_MXU),
        "wqt": jnp.swapaxes(w_query_peer, 1, 2).astype(_MXU),
        "sk1": subkeys1_peer, "sk2": subkeys2_peer,
        "u": u_peer.astype(_MXU), "vt": jnp.swapaxes(v_peer, 1, 2).astype(_MXU),
    }
    return _trunk(x_prompt, p), _trunk(x_sample, p)
```

```python
import functools

import numpy as np
import jax
import jax.numpy as jnp
from jax import lax
from jax.experimental import pallas as pl
from jax.experimental.pallas import tpu as pltpu

D_MODEL = 1024
HEAD_DIM = 64
N_HEADS = 16
GRID_W = 64
NA_KH = 8
NA_KW = 16
NA_ROWS_PER_STEP = 4
NA_HEAD_LANES = 512
DIL_CONFIGS = ((128, 1), (512, 4), (2048, 16))
DIL_BLOCK = 256
DIL_HALO = 64
ROPE_THETA = 10000.0
PEER_HEADS = 8
N_KEYS = 128
N_EXPERTS = N_KEYS * N_KEYS
PEER_TOPK = 16
RMS_EPS = 1e-6
NEG_INF = -1e30
LANES = 128

_MXU = jnp.bfloat16
_GATE = jnp.bfloat16
_VMEM_LIMIT = 56 * 1024 * 1024

_NT = (((1,), (1,)), ((), ()))

_MARK = -(2.0 ** 127)
_MARK_STEP = 2.0 ** 104
_MARK_LIMIT = -(2.0 ** 126)
_SCORE_FLOOR = -(2.0 ** 120)


def _params(sem):
    return pltpu.CompilerParams(dimension_semantics=sem, vmem_limit_bytes=_VMEM_LIMIT)


def _rmsnorm(x, g):
    ms = jnp.mean(x * x, axis=-1, keepdims=True)
    return x * lax.rsqrt(ms + RMS_EPS) * g


def _norm_proj_kernel(*refs, rope, dil):
    refs = list(refs)
    x_ref, g_ref, w_ref = refs[:3]
    cos_ref, sin_ref = (refs[3], refs[4]) if rope else (None, None)
    o_ref, xn_ref = refs[5:7] if rope else refs[3:5]
    y_ref = refs[-1] if (rope or dil > 1) else None
    j = pl.program_id(1)

    @pl.when(j == 0)
    def _():
        xn_ref[...] = _rmsnorm(x_ref[...], g_ref[...]).astype(xn_ref.dtype)

    acc = jnp.dot(xn_ref[...], w_ref[...], preferred_element_type=jnp.float32)
    scale = jnp.where(j == 0, HEAD_DIM ** -0.5, 1.0).astype(jnp.float32)

    if y_ref is None:
        o_ref[...] = (acc * scale).astype(o_ref.dtype)
        return

    n_lt = acc.shape[1] // LANES
    if rope:
        @pl.when(j == 2)
        def _():
            for t in range(n_lt):
                y_ref[t] = acc[:, t * LANES:(t + 1) * LANES]

        @pl.when(j != 2)
        def _():
            cos = cos_ref[...]
            sin = sin_ref[...]
            lane = lax.broadcasted_iota(jnp.int32, cos.shape, 1)
            first_half = (lane % HEAD_DIM) < HEAD_DIM // 2
            for t in range(n_lt):
                xt = acc[:, t * LANES:(t + 1) * LANES]
                partner = jnp.where(first_half,
                                    pltpu.roll(xt, LANES - HEAD_DIM // 2, 1),
                                    pltpu.roll(xt, HEAD_DIM // 2, 1))
                y_ref[t] = (xt * cos + partner * sin) * scale
    else:
        for t in range(n_lt):
            y_ref[t] = acc[:, t * LANES:(t + 1) * LANES] * scale

    rows = acc.shape[0] // dil
    for r in range(dil):
        for t in range(n_lt):
            src = y_ref[t] if dil == 1 else y_ref[t, pl.ds(r, rows, stride=dil), :]
            c0 = r * D_MODEL + t * LANES
            o_ref[:, c0:c0 + LANES] = src.astype(o_ref.dtype)


def _norm_proj(x2, g, w, seq_len, col0=0, dil=1, rope_tables=None, tm=1024):
    T, D = x2.shape
    rope = rope_tables is not None
    in_specs = [
        pl.BlockSpec((tm, D), lambda i, j: (i, 0)),
        pl.BlockSpec((1, D), lambda i, j: (0, 0)),
        pl.BlockSpec((D, D), lambda i, j: (0, col0 + j)),
    ]
    args = [x2, g, w]
    if rope:
        nper = seq_len // tm
        tab = pl.BlockSpec((tm, LANES), lambda i, j: (i % nper, 0))
        in_specs += [tab, tab]
        args += list(rope_tables)
    scratch = [pltpu.VMEM((tm, D), _MXU)]
    if rope or dil > 1:
        scratch.append(pltpu.VMEM((D // LANES, tm, LANES), jnp.float32))
    return pl.pallas_call(
        functools.partial(_norm_proj_kernel, rope=rope, dil=dil),
        grid=(T // tm, 3),
        in_specs=in_specs,
        out_specs=pl.BlockSpec((tm // dil, dil * D), lambda i, j: (i, j)),
        out_shape=jax.ShapeDtypeStruct((T // dil, 3 * dil * D), _MXU),
        scratch_shapes=scratch,
        compiler_params=_params(("parallel", "arbitrary")),
        name=f"norm_proj_rope_d{dil}" if rope else "norm_proj",
    )(*args)


def _rope_tables(seq_len):
    half = HEAD_DIM // 2
    freqs = ROPE_THETA ** (-jnp.arange(half, dtype=jnp.float32) / half)
    ang = jnp.arange(seq_len, dtype=jnp.float32)[:, None] * freqs[None, :]
    cos = jnp.cos(ang)
    sin = jnp.sin(ang)
    cos_t = jnp.tile(cos, (1, LANES // half))
    sin_t = jnp.tile(jnp.concatenate([-sin, sin], axis=1), (1, LANES // HEAD_DIM))
    return cos_t, sin_t


def _na_kernel(q_ref, k0_ref, k1_ref, k2_ref, v0_ref, v1_ref, v2_ref, bias_ref, o_ref,
               kbuf, vbuf, s_scr, m_scr, e_scr, *, rows):
    g = pl.program_id(2)
    rp = NA_ROWS_PER_STEP
    blk = rp * GRID_W
    nblk = rows // rp
    b0 = jnp.clip(g - 1, 0, nblk - 3)
    for i, (kr, vr) in enumerate(((k0_ref, v0_ref), (k1_ref, v1_ref), (k2_ref, v2_ref))):
        kbuf[i * blk:(i + 1) * blk, :] = kr[0]
        vbuf[i * blk:(i + 1) * blk, :] = vr[0]
    nkeys = NA_KH * GRID_W
    ones = jnp.ones((nkeys, LANES), kbuf.dtype)
    nh = q_ref.shape[2] // HEAD_DIM
    offs = []
    for r in range(rp):
        r_abs = g * rp + r
        rs = jnp.clip(r_abs - NA_KH // 2, 0, rows - NA_KH)
        off = pl.multiple_of((rs - b0 * rp) * GRID_W, GRID_W)
        offs.append(off)
        var = rs - r_abs + NA_KH - 1
        for h in range(nh):
            hs = slice(h * HEAD_DIM, (h + 1) * HEAD_DIM)
            qh = q_ref[0, r * GRID_W:(r + 1) * GRID_W, hs]
            s = lax.dot_general(qh, kbuf[pl.ds(off, nkeys), hs], _NT, preferred_element_type=jnp.float32)
            s = s + bias_ref[var, h]
            s_scr[r * nh + h] = s
            m_scr[r * nh + h] = jnp.broadcast_to(jnp.max(s, axis=-1, keepdims=True), (GRID_W, LANES))
    for u in range(rp * nh):
        mb = m_scr[u]
        e_scr[u] = jnp.exp(s_scr[u] - jnp.concatenate([mb] * (nkeys // LANES), axis=1)).astype(e_scr.dtype)
    for r in range(rp):
        for h in range(nh):
            hs = slice(h * HEAD_DIM, (h + 1) * HEAD_DIM)
            e = e_scr[r * nh + h]
            l = jnp.dot(e, ones, preferred_element_type=jnp.float32)
            o = jnp.dot(e, vbuf[pl.ds(offs[r], nkeys), hs], preferred_element_type=jnp.float32)
            o_ref[0, r * GRID_W:(r + 1) * GRID_W, hs] = (o / l[:, :HEAD_DIM]).astype(o_ref.dtype)


def _na_bias_table(rpb):
    qc = np.arange(GRID_W)[:, None]
    kc = np.arange(GRID_W)[None, :]
    cs = np.clip(qc - NA_KW // 2, 0, GRID_W - NA_KW)
    ok = (kc >= cs) & (kc < cs + NA_KW)
    col_neg = jnp.asarray(np.where(ok, 0.0, NEG_INF), jnp.float32)
    col_off = np.clip(kc - qc + NA_KW - 1, 0, 2 * NA_KW - 2)
    tab = rpb.astype(jnp.float32)[:, :, col_off] + col_neg[None, None]
    variants = []
    for var in range(NA_KH):
        t = tab[:, var:var + NA_KH]
        variants.append(jnp.transpose(t, (0, 2, 1, 3)).reshape(N_HEADS, GRID_W, NA_KH * GRID_W))
    return jnp.stack(variants, axis=0)


def _na_attention(qkv, bias_tab, B, S):
    rows = S // GRID_W
    rp = NA_ROWS_PER_STEP
    blk = rp * GRID_W
    nblk = rows // rp
    wl = NA_HEAD_LANES
    nhp = D_MODEL // wl
    hp_heads = wl // HEAD_DIM

    def kv_spec(which, i):
        return pl.BlockSpec(
            (1, blk, wl),
            lambda hp, b, g: (b, jnp.clip(g - 1, 0, nblk - 3) + i, which * nhp + hp))

    return pl.pallas_call(
        functools.partial(_na_kernel, rows=rows),
        grid=(nhp, B, nblk),
        in_specs=[pl.BlockSpec((1, blk, wl), lambda hp, b, g: (b, g, hp))]
        + [kv_spec(1, i) for i in range(3)] + [kv_spec(2, i) for i in range(3)]
        + [pl.BlockSpec((NA_KH, hp_heads, GRID_W, NA_KH * GRID_W), lambda hp, b, g: (0, hp, 0, 0))],
        out_specs=pl.BlockSpec((1, blk, wl), lambda hp, b, g: (b, g, hp)),
        out_shape=jax.ShapeDtypeStruct((B, S, D_MODEL), qkv.dtype),
        scratch_shapes=[pltpu.VMEM((3 * blk, wl), qkv.dtype),
                        pltpu.VMEM((3 * blk, wl), qkv.dtype),
                        pltpu.VMEM((rp * hp_heads, GRID_W, NA_KH * GRID_W), jnp.float32),
                        pltpu.VMEM((rp * hp_heads, GRID_W, LANES), jnp.float32),
                        pltpu.VMEM((rp * hp_heads, GRID_W, NA_KH * GRID_W), qkv.dtype)],
        compiler_params=_params(("arbitrary", "arbitrary", "arbitrary")),
        name="na_attention",
    )(qkv, qkv, qkv, qkv, qkv, qkv, qkv, bias_tab)


def _proj_res_kernel(o_ref, w_ref, x_ref, out_ref):
    out_ref[...] = x_ref[...] + jnp.dot(o_ref[...], w_ref[...], preferred_element_type=jnp.float32)


def _proj_residual(o2, w, x2, tm=512):
    T, D = x2.shape
    return pl.pallas_call(
        _proj_res_kernel,
        grid=(T // tm,),
        in_specs=[pl.BlockSpec((tm, D), lambda i: (i, 0)),
                  pl.BlockSpec((D, D), lambda i: (0, 0)),
                  pl.BlockSpec((tm, D), lambda i: (i, 0))],
        out_specs=pl.BlockSpec((tm, D), lambda i: (i, 0)),
        out_shape=jax.ShapeDtypeStruct((T, D), jnp.float32),
        compiler_params=_params(("parallel",)),
        name="proj_residual",
    )(o2, w, x2)


def _dil_kernel(q_ref, kp_ref, kc_ref, kn_ref, vp_ref, vc_ref, vn_ref, o_ref, lse_ref, kbuf, vbuf,
                s_scr, m_scr, e_scr, *,
                n_blocks, half):
    jb = pl.program_id(2)
    C = q_ref.shape[1]
    hc = kp_ref.shape[1]
    nk = C + 2 * hc
    for buf, (p_ref, c_ref, n_ref) in ((kbuf, (kp_ref, kc_ref, kn_ref)), (vbuf, (vp_ref, vc_ref, vn_ref))):
        buf[0:hc, :] = p_ref[0]
        buf[hc:hc + C, :] = c_ref[0]
        buf[hc + C:nk, :] = n_ref[0]
    qi = lax.broadcasted_iota(jnp.int32, (C, nk), 0)
    ci = lax.broadcasted_iota(jnp.int32, (C, nk), 1)
    lo = jnp.where(jb == 0, hc, 0)
    hi = jnp.where(jb == n_blocks - 1, hc + C, nk)
    mask = (jnp.abs(ci - hc - qi) <= half) & (ci >= lo) & (ci < hi)
    ones = jnp.ones((nk, LANES), kbuf.dtype)
    lane = lax.broadcasted_iota(jnp.int32, (C, LANES), 1)
    lse_all = jnp.zeros((C, LANES), jnp.float32)
    for h in range(N_HEADS):
        hs = slice(h * HEAD_DIM, (h + 1) * HEAD_DIM)
        s = lax.dot_general(q_ref[0, :, hs], kbuf[:, hs], _NT, preferred_element_type=jnp.float32)
        s = jnp.where(mask, s, NEG_INF)
        s_scr[h] = s
        mt = s[:, :LANES]
        for t in range(1, nk // LANES):
            mt = jnp.maximum(mt, s[:, t * LANES:(t + 1) * LANES])
        m_scr[h] = jnp.broadcast_to(jnp.max(mt, axis=-1, keepdims=True), (C, LANES))
    for h in range(N_HEADS):
        mb = m_scr[h]
        e_scr[h] = jnp.exp(s_scr[h] - jnp.concatenate([mb] * (nk // LANES), axis=1)).astype(e_scr.dtype)
    for h in range(N_HEADS):
        hs = slice(h * HEAD_DIM, (h + 1) * HEAD_DIM)
        e = e_scr[h]
        l = jnp.dot(e, ones, preferred_element_type=jnp.float32)
        o = jnp.dot(e, vbuf[:, hs], preferred_element_type=jnp.float32)
        o_ref[0, :, hs] = (o / l[:, :HEAD_DIM]).astype(o_ref.dtype)
        lse_all = jnp.where(lane == h, m_scr[h] + jnp.log(l), lse_all)
    lse_ref[0] = lse_all


def _dil_group(qkv_g, B, S, window, dil):
    n = S // dil
    C = DIL_BLOCK
    hc = DIL_HALO
    nb = n // C
    half = window // (2 * dil)
    assert half <= hc and n % C == 0 and C % hc == 0
    per = C // hc
    qkv_v = qkv_g.reshape(B, n, qkv_g.shape[-1])

    def cur(which):
        return pl.BlockSpec((1, C, D_MODEL), lambda b, r, jb: (b, jb, which * dil + r))

    def edge(which, shift):
        def idx(b, r, jb):
            return (b, jnp.clip(per * jb + shift, 0, per * nb - 1), which * dil + r)
        return pl.BlockSpec((1, hc, D_MODEL), idx)

    o, lse = pl.pallas_call(
        functools.partial(_dil_kernel, n_blocks=nb, half=half),
        grid=(B, dil, nb),
        in_specs=[cur(0), edge(1, -1), cur(1), edge(1, per), edge(2, -1), cur(2), edge(2, per)],
        out_specs=[pl.BlockSpec((1, C, D_MODEL), lambda b, r, jb: (b, jb, r)),
                   pl.BlockSpec((1, C, LANES), lambda b, r, jb: (b, jb, r))],
        out_shape=[jax.ShapeDtypeStruct((B, n, dil * D_MODEL), jnp.float32),
                   jax.ShapeDtypeStruct((B, n, dil * LANES), jnp.float32)],
        scratch_shapes=[pltpu.VMEM((C + 2 * hc, D_MODEL), qkv_g.dtype),
                        pltpu.VMEM((C + 2 * hc, D_MODEL), qkv_g.dtype),
                        pltpu.VMEM((N_HEADS, C, C + 2 * hc), jnp.float32),
                        pltpu.VMEM((N_HEADS, C, LANES), jnp.float32),
                        pltpu.VMEM((N_HEADS, C, C + 2 * hc), qkv_g.dtype)],
        compiler_params=_params(("parallel", "parallel", "arbitrary")),
        name=f"dil_group_d{dil}",
    )(qkv_v, qkv_v, qkv_v, qkv_v, qkv_v, qkv_v, qkv_v)
    return o.reshape(B * n, dil * D_MODEL), lse.reshape(B * n, dil * LANES)


def _dil_merge_kernel(o1_ref, o2_ref, o3_ref, l1_ref, l2_ref, l3_ref, x_ref, w_ref, exp_ref, out_ref,
                      o_scr, l_scr):
    tm = x_ref.shape[0]
    outs, lses = [], []
    for gi, (o_ref, l_ref) in enumerate(((o1_ref, l1_ref), (o2_ref, l2_ref), (o3_ref, l3_ref))):
        dil = DIL_CONFIGS[gi][1]
        if dil == 1:
            outs.append(o_ref[...])
            lses.append(l_ref[...])
            continue
        rows = tm // dil
        n_lt = D_MODEL // LANES
        for r in range(dil):
            for t in range(n_lt):
                c0 = r * D_MODEL + t * LANES
                o_scr[gi, t, pl.ds(r, rows, stride=dil), :] = o_ref[:, c0:c0 + LANES]
            l_scr[gi, pl.ds(r, rows, stride=dil), :] = l_ref[:, r * LANES:(r + 1) * LANES]
        outs.append(jnp.concatenate([o_scr[gi, t] for t in range(n_lt)], axis=1))
        lses.append(l_scr[gi])
    m = jnp.maximum(jnp.maximum(lses[0], lses[1]), lses[2])
    es = [jnp.exp(l - m) for l in lses]
    z = es[0] + es[1] + es[2]
    merged = None
    for e, o in zip(es, outs):
        w = e / z
        hi = w.astype(_MXU)
        lo = (w - hi.astype(jnp.float32)).astype(_MXU)
        wide = (jnp.dot(hi, exp_ref[...], preferred_element_type=jnp.float32)
                + jnp.dot(lo, exp_ref[...], preferred_element_type=jnp.float32))
        term = wide * o
        merged = term if merged is None else merged + term
    out_ref[...] = x_ref[...] + jnp.dot(merged.astype(_MXU), w_ref[...],
                                        preferred_element_type=jnp.float32)


def _dil_merge_proj(outs, lses, x2, w, tm=256):
    T, D = x2.shape
    expand = np.zeros((LANES, D), np.float32)
    for h in range(N_HEADS):
        expand[h, h * HEAD_DIM:(h + 1) * HEAD_DIM] = 1.0
    expand = jnp.asarray(expand, _MXU)
    row = pl.BlockSpec((tm, D), lambda i: (i, 0))
    o_specs = [pl.BlockSpec((tm // d, d * D), lambda i: (i, 0)) for _, d in DIL_CONFIGS]
    l_specs = [pl.BlockSpec((tm // d, d * LANES), lambda i: (i, 0)) for _, d in DIL_CONFIGS]
    ng = len(DIL_CONFIGS)
    return pl.pallas_call(
        _dil_merge_kernel,
        grid=(T // tm,),
        in_specs=o_specs + l_specs + [row, pl.BlockSpec((D, D), lambda i: (0, 0)),
                                      pl.BlockSpec((LANES, D), lambda i: (0, 0))],
        out_specs=row,
        out_shape=jax.ShapeDtypeStruct((T, D), jnp.float32),
        scratch_shapes=[pltpu.VMEM((ng, D // LANES, tm, LANES), jnp.float32),
                        pltpu.VMEM((ng, tm, LANES), jnp.float32)],
        compiler_params=_params(("parallel",)),
        name="dil_merge_proj",
    )(*outs, *lses, x2, w, expand)


def _top16_rows(s, want_rank):
    K, L = s.shape
    kio = lax.broadcasted_iota(jnp.int32, (K, L), 0).astype(jnp.float32)
    aio = lax.broadcasted_iota(jnp.int32, (PEER_TOPK, L), 0)
    vals = jnp.zeros((PEER_TOPK, L), jnp.float32)
    idxs = jnp.zeros((PEER_TOPK, L), jnp.float32)
    rank = jnp.full((K, L), float(PEER_TOPK), jnp.float32) if want_rank else None
    for a in range(PEER_TOPK):
        m = jnp.max(s, axis=0, keepdims=True)
        idx = jnp.min(jnp.where(s == m, kio, float(K)), axis=0, keepdims=True)
        sel = kio == idx
        if want_rank:
            rank = jnp.where(sel, float(a), rank)
        s = jnp.where(sel, -jnp.inf, s)
        vals = jnp.where(aio == a, m, vals)
        idxs = jnp.where(aio == a, idx, idxs)
    return vals, idxs, rank


def _pair_candidates(v1, v2):
    L = v1.shape[1]
    K = PEER_TOPK
    blocks = [v1 + v2[0:1]]
    for b in range(1, 8):
        blocks.append(v1[0:8] + v2[b:b + 1])
    blocks.append(v1[0:1] + v2[8:16])
    cand = jnp.concatenate(blocks, axis=0)
    R = cand.shape[0]
    r = lax.broadcasted_iota(jnp.int32, (R, L), 0)
    a_mid = (r - 16) & 7
    b_mid = ((r - 16) >> 3) + 1
    a_of = jnp.where(r < 16, r, jnp.where(r < 72, a_mid, 0))
    b_of = jnp.where(r < 16, 0, jnp.where(r < 72, b_mid, r - 64))
    flat = (a_of * K + b_of).astype(jnp.float32)
    cand = jnp.where((a_of + 1) * (b_of + 1) <= K, cand, -jnp.inf)
    return cand, flat, a_of.astype(jnp.float32)


def _pair_counts(v1, v2):
    L = v1.shape[1]
    K = PEER_TOPK
    cand, flat, a_of_f = _pair_candidates(v1, v2)
    aio = lax.broadcasted_iota(jnp.int32, (K, L), 0).astype(jnp.float32)
    cnt = jnp.zeros((K, L), jnp.float32)
    top = v1[0:1] + v2[0:1]
    z = jnp.zeros((1, L), jnp.float32)
    for _ in range(K):
        m = jnp.max(cand, axis=0, keepdims=True)
        f = jnp.min(jnp.where(cand == m, flat, float(K * K)), axis=0, keepdims=True)
        sel = flat == f
        a_sel = jnp.max(jnp.where(sel, a_of_f, 0.0), axis=0, keepdims=True)
        cand = jnp.where(sel, -jnp.inf, cand)
        cnt = cnt + jnp.where(aio == a_sel, 1.0, 0.0)
        z = z + jnp.exp(m - top)
    return cnt, z


def _route_head(s1, s2):
    v1, i1, _ = _top16_rows(s1, False)
    v2, _, rank2 = _top16_rows(s2, True)
    cnt, z = _pair_counts(v1, v2)
    kio = lax.broadcasted_iota(jnp.int32, s1.shape, 0).astype(jnp.float32)
    cnt_i = jnp.zeros_like(s1)
    for a in range(PEER_TOPK):
        cnt_i = jnp.where(kio == i1[a:a + 1], cnt[a:a + 1], cnt_i)
    e1 = jnp.exp(s1 - v1[0:1])
    e2 = jnp.exp(s2 - v2[0:1]) / z
    return cnt_i, e1, rank2, e2


def _mark_top16(s):
    L = s.shape[1]
    aio = lax.broadcasted_iota(jnp.int32, (PEER_TOPK, L), 0)
    vals = jnp.zeros((PEER_TOPK, L), jnp.float32)
    for a in range(PEER_TOPK):
        m = jnp.max(s, axis=0, keepdims=True)
        s = jnp.where(s == m, _MARK + a * _MARK_STEP, s)
        vals = jnp.where(aio == a, m, vals)
    marked = s < _MARK_LIMIT
    rank = jnp.where(marked, (s - _MARK) * (1.0 / _MARK_STEP), float(PEER_TOPK))
    n_marked = jnp.sum(jnp.where(marked, 1.0, 0.0), axis=0, keepdims=True)
    return vals, rank, n_marked


def _pair_counts_distinct(v1, v2):
    L = v1.shape[1]
    K = PEER_TOPK
    orig, _, _ = _pair_candidates(v1, v2)
    cand = orig
    for _ in range(K):
        m = jnp.max(cand, axis=0, keepdims=True)
        cand = jnp.where(cand == m, _MARK, cand)
    sel = cand == _MARK
    self = jnp.where(sel, 1.0, 0.0)
    c8 = self[16:24]
    for b in range(2, 8):
        c8 = c8 + self[8 + 8 * b:16 + 8 * b]
    row8 = lax.broadcasted_iota(jnp.int32, (8, L), 0)
    c8 = c8 + jnp.where(row8 == 0, jnp.sum(self[72:80], axis=0, keepdims=True), 0.0)
    cnt = self[0:16] + jnp.concatenate([c8, jnp.zeros((8, L), jnp.float32)], axis=0)
    top = v1[0:1] + v2[0:1]
    z = jnp.sum(jnp.where(sel, jnp.exp(orig - top), 0.0), axis=0, keepdims=True)
    return cnt, z, jnp.sum(self, axis=0, keepdims=True)


def _route_head_distinct(s1, s2):
    v1, rank1, n1 = _mark_top16(s1)
    v2, rank2, n2 = _mark_top16(s2)
    cnt, z, n3 = _pair_counts_distinct(v1, v2)
    cnt_i = jnp.zeros_like(s1)
    for a in range(PEER_TOPK):
        cnt_i = jnp.where(rank1 == float(a), cnt[a:a + 1], cnt_i)
    e1 = jnp.exp(s1 - v1[0:1])
    e2 = jnp.exp(s2 - v2[0:1]) / z
    k = float(PEER_TOPK)
    floor = jnp.minimum(jnp.min(s1, axis=0, keepdims=True), jnp.min(s2, axis=0, keepdims=True))
    ok = (n1 == k) & (n2 == k) & (n3 == k) & (floor > _SCORE_FLOOR)
    return (cnt_i, e1, rank2, e2), ok


def _router_kernel(x_ref, gb_ref, wqt_ref, k1h_ref, k1l_ref, k2h_ref, k2l_ref,
                   hnt_ref, row_ref, rank2_ref, e2_ref, qh_scr, ql_scr):
    xt = x_ref[...].T
    ms = jnp.mean(xt * xt, axis=0, keepdims=True)
    hnt = (xt * lax.rsqrt(ms + RMS_EPS) * gb_ref[...]).astype(hnt_ref.dtype)
    hnt_ref[...] = hnt
    qt = jnp.dot(wqt_ref[...], hnt, preferred_element_type=jnp.float32)
    q_hi = qt.astype(_MXU)
    qh_scr[...] = q_hi
    ql_scr[...] = (qt - q_hi.astype(jnp.float32)).astype(_MXU)
    L = xt.shape[1]

    heads_per_iter = 4

    def scores(kh_ref, kl_ref, h, r0, ls):
        qh = qh_scr[pl.ds(r0, N_KEYS), ls]
        ql = ql_scr[pl.ds(r0, N_KEYS), ls]
        return (jnp.dot(kh_ref[h], qh, preferred_element_type=jnp.float32)
                + jnp.dot(kh_ref[h], ql, preferred_element_type=jnp.float32)
                + jnp.dot(kl_ref[h], qh, preferred_element_type=jnp.float32))

    def store(res, h, ls):
        cnt_i, e1, rank2, e2 = res
        row_ref[2 * h, :, ls] = cnt_i
        row_ref[2 * h + 1, :, ls] = e1
        rank2_ref[h, :, ls] = rank2.astype(rank2_ref.dtype)
        e2_ref[h, :, ls] = e2.astype(e2_ref.dtype)

    def head_group(hg, carry):
        units = []
        for hh in range(heads_per_iter):
            h = hg * heads_per_iter + hh
            r1 = pl.multiple_of(h * 2 * N_KEYS, 2 * N_KEYS)
            for lt in range(L // LANES):
                ls = slice(lt * LANES, (lt + 1) * LANES)
                s1 = scores(k1h_ref, k1l_ref, h, r1, ls)
                s2 = scores(k2h_ref, k2l_ref, h, r1 + N_KEYS, ls)
                res, ok = _route_head_distinct(s1, s2)
                store(res, h, ls)
                units.append((h, ls, s1, s2, jnp.sum(jnp.where(ok, 0.0, 1.0))))
        for h, ls, s1, s2, n_bad in units:
            @pl.when(n_bad > 0.0)
            def _():
                store(_route_head(s1, s2), h, ls)
        return carry

    lax.fori_loop(0, PEER_HEADS // heads_per_iter, head_group, 0)


def _peer_router(x2, g, wqt, k1h, k1l, k2h, k2l, tm=256):
    T, D = x2.shape
    NQ = wqt.shape[0]
    H = PEER_HEADS
    gb = jnp.broadcast_to(g.reshape(D, 1), (D, tm))
    kspec = pl.BlockSpec((H, N_KEYS, N_KEYS), lambda i: (0, 0, 0))
    return pl.pallas_call(
        _router_kernel,
        grid=(T // tm,),
        in_specs=[pl.BlockSpec((tm, D), lambda i: (i, 0)),
                  pl.BlockSpec((D, tm), lambda i: (0, 0)),
                  pl.BlockSpec((NQ, D), lambda i: (0, 0)),
                  kspec, kspec, kspec, kspec],
        out_specs=[pl.BlockSpec((D, tm), lambda i: (0, i)),
                   pl.BlockSpec((2 * H, N_KEYS, tm), lambda i: (0, 0, i)),
                   pl.BlockSpec((H, N_KEYS, tm), lambda i: (0, 0, i)),
                   pl.BlockSpec((H, N_KEYS, tm), lambda i: (0, 0, i))],
        out_shape=[jax.ShapeDtypeStruct((D, T), _MXU),
                   jax.ShapeDtypeStruct((2 * H, N_KEYS, T), jnp.float32),
                   jax.ShapeDtypeStruct((H, N_KEYS, T), _GATE),
                   jax.ShapeDtypeStruct((H, N_KEYS, T), _GATE)],
        scratch_shapes=[pltpu.VMEM((NQ, tm), _MXU), pltpu.VMEM((NQ, tm), _MXU)],
        compiler_params=_params(("parallel",)),
        name="peer_router",
    )(x2, gb, wqt, k1h, k1l, k2h, k2l)


def _gelu(a):
    return 0.5 * a * (1.0 + lax.erf(a * (2.0 ** -0.5)))


def _peer_dense_kernel(*refs, n_e, final_norm):
    refs = list(refs)
    hnt_ref, u_ref, vt_ref, row_ref, rank2_ref, e2_ref, x_ref = refs[:7]
    gf_ref = refs[7] if final_norm else None
    out_ref, acc_ref, gt_ref = refs[-3:]
    s = pl.program_id(0)
    n_pairs = pl.num_programs(0) - 1
    eb = u_ref.shape[0]
    sub = eb // N_KEYS

    @pl.when(s == 0)
    def _():
        gt_ref[...] = jnp.zeros_like(gt_ref)
        acc_ref[...] = jnp.zeros_like(acc_ref)

    slot = s % 2
    e_a = jnp.minimum(s, n_pairs - 1) % n_e
    e_y = jnp.clip(s - 1, 0, n_pairs - 1) % n_e

    acc_ref[...] += jnp.dot(vt_ref[...], gt_ref[1 - slot], preferred_element_type=jnp.float32)

    at = jnp.dot(u_ref[...], hnt_ref[...], preferred_element_type=jnp.float32)
    for ii in range(sub):
        i = e_a * sub + ii
        rs = slice(ii * N_KEYS, (ii + 1) * N_KEYS)
        gate = None
        for h in range(PEER_HEADS):
            cnt = row_ref[2 * h, pl.ds(i, 1), :].astype(_GATE)
            e1 = row_ref[2 * h + 1, pl.ds(i, 1), :].astype(_GATE)
            term = jnp.where(rank2_ref[h] < cnt, e2_ref[h], jnp.zeros((), _GATE)) * e1
            gate = term if gate is None else gate + term
        gt_ref[slot, rs, :] = (gate * _gelu(at[rs, :]).astype(_GATE)).astype(gt_ref.dtype)

    @pl.when((e_y == n_e - 1) & (s >= 1))
    def _():
        res = x_ref[...] + acc_ref[...].T
        if final_norm:
            res = _rmsnorm(res, gf_ref[...])
        out_ref[...] = res
        acc_ref[...] = jnp.zeros_like(acc_ref)


def _peer_dense(hnt, u, vt, rowdat, rank2, e2, x2, g_final=None, tm=512, eb=2048):
    T, D = x2.shape
    H = PEER_HEADS
    n_e = N_EXPERTS // eb
    n_pairs = (T // tm) * n_e
    final_norm = g_final is not None

    def pair(s, lag):
        p = jnp.clip(s - lag, 0, n_pairs - 1)
        return p // n_e, p % n_e

    in_specs = [pl.BlockSpec((D, tm), lambda s: (0, pair(s, 0)[0])),
                pl.BlockSpec((eb, D), lambda s: (pair(s, 0)[1], 0)),
                pl.BlockSpec((D, eb), lambda s: (0, pair(s, 1)[1])),
                pl.BlockSpec((2 * H, N_KEYS, tm), lambda s: (0, 0, pair(s, 0)[0])),
                pl.BlockSpec((H, N_KEYS, tm), lambda s: (0, 0, pair(s, 0)[0])),
                pl.BlockSpec((H, N_KEYS, tm), lambda s: (0, 0, pair(s, 0)[0])),
                pl.BlockSpec((tm, D), lambda s: (pair(s, 1)[0], 0))]
    args = [hnt, u, vt, rowdat, rank2, e2, x2]
    if final_norm:
        in_specs.append(pl.BlockSpec((1, D), lambda s: (0, 0)))
        args.append(g_final)
    return pl.pallas_call(
        functools.partial(_peer_dense_kernel, n_e=n_e, final_norm=final_norm),
        grid=(n_pairs + 1,),
        in_specs=in_specs,
        out_specs=pl.BlockSpec((tm, D), lambda s: (pair(s, 1)[0], 0)),
        out_shape=jax.ShapeDtypeStruct((T, D), jnp.float32),
        scratch_shapes=[pltpu.VMEM((D, tm), jnp.float32),
                        pltpu.VMEM((2, eb, tm), _MXU)],
        compiler_params=_params(("arbitrary",)),
        name="peer_dense_final" if final_norm else "peer_dense",
    )(*args)


def _split_hi_lo(w):
    hi = w.astype(_MXU)
    lo = (w - hi.astype(jnp.float32)).astype(_MXU)
    return hi, lo


def _peer(x2, g, wqt, sk1, sk2, u, vt, g_final=None):
    k1h, k1l = _split_hi_lo(sk1)
    k2h, k2l = _split_hi_lo(sk2)
    hnt, rowdat, rank2, e2 = _peer_router(x2, g, wqt, k1h, k1l, k2h, k2l)
    return _peer_dense(hnt, u, vt, rowdat, rank2, e2, x2, g_final)


def _trunk(x, p):
    B, S, D = x.shape
    x2 = x.reshape(B * S, D)
    qkv = _norm_proj(x2, p["ln_mix"][0:1], p["w_qkv_na"], S)
    o = _na_attention(qkv.reshape(B, S, 3 * D), p["na_bias"], B, S)
    x2 = _proj_residual(o.reshape(B * S, D), p["w_o_na"], x2)
    x2 = _peer(x2, p["ln_ffn"][0:1], p["wqt"][0], p["sk1"][0], p["sk2"][0], p["u"][0], p["vt"][0])
    tables = _rope_tables(S)
    outs, lses = [], []
    for gi, (window, dil) in enumerate(DIL_CONFIGS):
        qkv_g = _norm_proj(x2, p["ln_mix"][1:2], p["w_qkv_dil"], S, col0=3 * gi, dil=dil, rope_tables=tables)
        o_g, lse_g = _dil_group(qkv_g, B, S, window, dil)
        outs.append(o_g)
        lses.append(lse_g)
    x2 = _dil_merge_proj(outs, lses, x2, p["w_o_dil"])
    x2 = _peer(x2, p["ln_ffn"][1:2], p["wqt"][1], p["sk1"][1], p["sk2"][1], p["u"][1], p["vt"][1],
               g_final=p["ln_final"])
    return x2.reshape(B, S, D)


def kernel(x_prompt, x_sample, ln_mix, ln_ffn, ln_final, w_qkv_na, rpb_na, w_o_na, w_qkv_dil, w_o_dil,
           w_query_peer, subkeys1_peer, subkeys2_peer, u_peer, v_peer):
    p = {
        "ln_mix": ln_mix, "ln_ffn": ln_ffn, "ln_final": ln_final.reshape(1, -1),
        "w_qkv_na": w_qkv_na[0].astype(_MXU), "w_o_na": w_o_na[0].astype(_MXU),
        "na_bias": _na_bias_table(rpb_na[0]),
        "w_qkv_dil": w_qkv_dil[0].astype(_MXU), "w_o_dil": w_o_dil[0].astype(_MXU),
        "wqt": jnp.swapaxes(w_query_peer, 1, 2).astype(_MXU),
        "sk1": subkeys1_peer, "sk2": subkeys2_peer,
        "u": u_peer.astype(_MXU), "vt": jnp.swapaxes(v_peer, 1, 2).astype(_MXU),
    }
    return _trunk(x_prompt, p), _trunk(x_sample, p)
```
